```python
import math
import jax, jax.numpy as jnp
from jax import lax
import numpy as np

D_MODEL = 2048
BATCH = 32
SEQ = 256
DEPTH = 2
DEC_BATCH = 2
DEC_SEQ = 4096
PAST_LEN = 512

GRID_W = 64
HEAD_DIM = 128
NA_HEADS = 8
NA_WIDTH = NA_HEADS * HEAD_DIM
NA_WIN_R = 8
NA_WIN_C = 16
S5_WIDTH = 1024
S5_GROUP = 16
S5_GROUPS = S5_WIDTH // S5_GROUP
S5_STATE = 64
AB_IN = 3 * NA_WIDTH + S5_WIDTH
AB_OUT = NA_WIDTH + S5_WIDTH
SWA_HEADS = 16
SWA_KV_HEADS = 4
SWA_Q = SWA_HEADS * HEAD_DIM
SWA_KV = SWA_KV_HEADS * HEAD_DIM
SWA_IN = SWA_Q + 2 * SWA_KV
SWA_WINDOW = 128
SWA_BLOCK = 128
ROPE_BASE = 10000.0
D_FF = 5632
CONV_W = 3
ATTN_QBLOCK = 128
EPS = 1e-6
NEG_INF = -1e30

kernel_name = 'hybrid_prefix_dit_na_s5_swa_step'


def rmsnorm(x, g):
    xf = x.astype(jnp.float32)
    y = xf * lax.rsqrt(jnp.mean(xf * xf, axis=-1, keepdims=True) + EPS)
    return (y * g.astype(jnp.float32)).astype(x.dtype)


def adaln(cond, w, b):
    m = jax.nn.silu(cond) @ w + b
    m = m.reshape(cond.shape[0], 1, 6, cond.shape[-1])
    return [m[:, :, j] for j in range(6)]


def modulate(x, g, shift, scale):
    return rmsnorm(x, g) * (1 + scale) + shift


def axial_rope(x):
    l, dh = x.shape[1], x.shape[-1]
    t = jnp.arange(l)
    half = dh // 2
    freqs = ROPE_BASE ** (-jnp.arange(0, half, 2, dtype=jnp.float32) / half)
    xf = x.astype(jnp.float32)

    def rot(xp, pos):
        ang = pos.astype(jnp.float32)[:, None] * freqs[None, :]
        cos = jnp.cos(ang)[None, :, None, :]
        sin = jnp.sin(ang)[None, :, None, :]
        x1, x2 = xp[..., :half // 2], xp[..., half // 2:]
        return jnp.concatenate([x1 * cos - x2 * sin, x1 * sin + x2 * cos], axis=-1)

    out = jnp.concatenate([rot(xf[..., :half], t // GRID_W), rot(xf[..., half:], t % GRID_W)], axis=-1)
    return out.astype(x.dtype)


def ctx_self_attention(q, k, v, sink=None):
    bsz, s, h, dh = q.shape
    hkv = k.shape[2]
    grp = h // hkv
    nb = s // ATTN_QBLOCK
    qb = q.reshape(bsz, nb, ATTN_QBLOCK, hkv, grp, dh).transpose(1, 0, 2, 3, 4, 5)
    scale = dh ** -0.5

    def block(qi):
        sc = jnp.einsum('bqkgd,bskd->bkgqs', qi, k).astype(jnp.float32) * scale
        if sink is not None:
            sk = jnp.broadcast_to(sink.astype(jnp.float32).reshape(hkv, grp, 1, 1), sc.shape[:-1] + (1,))
            sc = jnp.concatenate([sc, sk], axis=-1)
        p = jax.nn.softmax(sc, axis=-1)[..., :s].astype(v.dtype)
        return jnp.einsum('bkgqs,bskd->bqkgd', p, v)

    out = lax.map(block, qb)
    return out.transpose(1, 0, 2, 3, 4, 5).reshape(bsz, s, h, dh)


def neighbourhood_attention(q, k, v, k_ctx, v_ctx, rpb):
    bsz, l, h, dh = q.shape
    rows = l // GRID_W
    kr = min(NA_WIN_R, rows)
    kc = NA_WIN_C
    r = jnp.arange(rows)
    key_rows = jnp.clip(r - kr // 2, 0, rows - kr)[:, None] + jnp.arange(kr)[None, :]
    col = jnp.arange(GRID_W)
    col_start = jnp.clip(col - kc // 2, 0, GRID_W - kc)
    col_ok = (col[None, :] >= col_start[:, None]) & (col[None, :] < col_start[:, None] + kc)
    qg = q.reshape(bsz, rows, GRID_W, h, dh)
    kg = k.reshape(bsz, rows, GRID_W, h, dh)[:, key_rows]
    vg = v.reshape(bsz, rows, GRID_W, h, dh)[:, key_rows]
    dr_idx = key_rows - r[:, None] + NA_WIN_R - 1
    dc_idx = jnp.clip(col[None, :] - col[:, None] + NA_WIN_C - 1, 0, 2 * NA_WIN_C - 2)
    bias = rpb[:, dr_idx[:, None, :, None], dc_idx[None, :, None, :]].astype(jnp.float32)
    scale = dh ** -0.5
    s_loc = jnp.einsum('brqhd,brjwhd->bhrqjw', qg, kg).astype(jnp.float32) * scale + bias[None]
    s_loc = jnp.where(col_ok[:, None, :], s_loc, NEG_INF)
    s_ctx = jnp.einsum('brqhd,bchd->bhrqc', qg, k_ctx).astype(jnp.float32) * scale
    n_loc = kr * GRID_W
    sc = jnp.concatenate([s_loc.reshape(bsz, h, rows, GRID_W, n_loc), s_ctx], axis=-1)
    p = jax.nn.softmax(sc, axis=-1).astype(v.dtype)
    p_loc = p[..., :n_loc].reshape(bsz, h, rows, GRID_W, kr, GRID_W)
    out = (jnp.einsum('bhrqjw,brjwhd->brqhd', p_loc, vg)
           + jnp.einsum('bhrqc,bchd->brqhd', p[..., n_loc:], v_ctx))
    return out.reshape(bsz, l, h, dh)


def window_attention(q, k, v, k_ctx, v_ctx, sink):
    bsz, l, h, dh = q.shape
    hkv = k.shape[2]
    grp = h // hkv
    blk = SWA_BLOCK
    nb = l // blk
    qb = q.reshape(bsz, nb, blk, hkv, grp, dh)
    pad = ((0, 0), (blk, blk), (0, 0), (0, 0))
    kp = jnp.pad(k, pad)
    vp = jnp.pad(v, pad)
    idx = jnp.arange(nb)[:, None] * blk + jnp.arange(3 * blk)[None, :]
    kb = kp[:, idx]
    vb = vp[:, idx]
    qpos = jnp.arange(nb)[:, None] * blk + jnp.arange(blk)[None, :]
    kpos = idx - blk
    ok = ((jnp.abs(qpos[:, :, None] - kpos[:, None, :]) <= SWA_WINDOW)
          & (kpos[:, None, :] >= 0) & (kpos[:, None, :] < l))
    scale = dh ** -0.5
    s_loc = jnp.einsum('bnqkgd,bnskd->bkgnqs', qb, kb).astype(jnp.float32) * scale
    s_loc = jnp.where(ok, s_loc, NEG_INF)
    s_ctx = jnp.einsum('bnqkgd,bckd->bkgnqc', qb, k_ctx).astype(jnp.float32) * scale
    s_sink = jnp.broadcast_to(sink.astype(jnp.float32).reshape(hkv, grp, 1, 1, 1), s_loc.shape[:-1] + (1,))
    p = jax.nn.softmax(jnp.concatenate([s_loc, s_ctx, s_sink], axis=-1), axis=-1).astype(v.dtype)
    n_loc = 3 * blk
    n_ctx = k_ctx.shape[1]
    out = (jnp.einsum('bkgnqs,bnskd->bnqkgd', p[..., :n_loc], vb)
           + jnp.einsum('bkgnqc,bckd->bnqkgd', p[..., n_loc:n_loc + n_ctx], v_ctx))
    return out.reshape(bsz, l, h, dh)


def _complex_affine_combine(e1, e2):
    a1r, a1i, b1r, b1i = e1
    a2r, a2i, b2r, b2i = e2
    return (a2r * a1r - a2i * a1i,
            a2r * a1i + a2i * a1r,
            a2r * b1r - a2i * b1i + b2r,
            a2r * b1i + a2i * b1r + b2i)


def s5_discretise(a_re, a_im, log_dt, b_re, b_im):
    dt = jnp.exp(log_dt)[:, None]
    mag = jnp.exp(a_re * dt)
    ang = a_im * dt
    ab_re = mag * jnp.cos(ang)
    ab_im = mag * jnp.sin(ang)
    den = a_re * a_re + a_im * a_im
    n_re = ab_re - 1.0
    f_re = (n_re * a_re + ab_im * a_im) / den
    f_im = (ab_im * a_re - n_re * a_im) / den
    bb_re = f_re[..., None] * b_re - f_im[..., None] * b_im
    bb_im = f_re[..., None] * b_im + f_im[..., None] * b_re
    return ab_re, ab_im, bb_re, bb_im


def s5_scan(u, ab_re, ab_im, bb_re, bb_im, h0_re, h0_im, reverse):
    bu_re = jnp.einsum('blgc,gpc->blgp', u, bb_re)
    bu_im = jnp.einsum('blgc,gpc->blgp', u, bb_im)
    first = -1 if reverse else 0
    bu_re = bu_re.at[:, first].add(ab_re * h0_re - ab_im * h0_im)
    bu_im = bu_im.at[:, first].add(ab_re * h0_im + ab_im * h0_re)
    a_re = jnp.broadcast_to(ab_re, bu_re.shape)
    a_im = jnp.broadcast_to(ab_im, bu_im.shape)
    _, _, x_re, x_im = lax.associative_scan(_complex_affine_combine, (a_re, a_im, bu_re, bu_im),
                                            axis=1, reverse=reverse)
    return x_re, x_im


def s5_mixer(u, a_re, a_im, log_dt, b_re, b_im, c_re, c_im, d, glu_w, glu_b, h0, want_final):
    bsz, l, _ = u.shape
    f32 = jnp.float32
    uf = u.astype(f32)
    ug = uf.reshape(bsz, l, S5_GROUPS, S5_GROUP)
    h0 = h0.astype(f32)
    y = d.astype(f32) * uf
    finals = []
    for direction in range(2):
        reverse = direction == 1
        ab_re, ab_im, bb_re, bb_im = s5_discretise(a_re[direction].astype(f32), a_im[direction].astype(f32),
                                                   log_dt[direction].astype(f32), b_re[direction].astype(f32),
                                                   b_im[direction].astype(f32))
        x_re, x_im = s5_scan(ug, ab_re, ab_im, bb_re, bb_im, h0[:, direction, :, :, 0], h0[:, direction, :, :, 1], reverse)
        yd = (jnp.einsum('blgp,gcp->blgc', x_re, c_re[direction].astype(f32))
              - jnp.einsum('blgp,gcp->blgc', x_im, c_im[direction].astype(f32)))
        y = y + yd.reshape(bsz, l, S5_WIDTH)
        if want_final:
            last = 0 if reverse else l - 1
            finals.append(jnp.stack([x_re[:, last], x_im[:, last]], axis=-1))
    g = jax.nn.gelu(y)
    out = (g * jax.nn.sigmoid(g @ glu_w.astype(f32) + glu_b.astype(f32))).astype(u.dtype)
    if want_final:
        return out, jnp.stack(finals, axis=1)
    return out


def conv_ffn(h, w_up, conv_w, conv_b, w_down):
    z = h @ w_up
    z = lax.conv_general_dilated(z, conv_w[:, None, :].astype(z.dtype), window_strides=(1,),
                                 padding=[(CONV_W // 2, CONV_W // 2)],
                                 dimension_numbers=('NWC', 'WIO', 'NWC'),
                                 feature_group_count=z.shape[-1]) + conv_b
    g, u = jnp.split(z, 2, axis=-1)
    return (jax.nn.silu(g) * u) @ w_down


def setup_inputs(seed: int = 0) -> dict:
    key = jax.random.key(seed)
    keys = iter(jax.random.split(key, 48))
    ne, no = (DEPTH + 1) // 2, DEPTH // 2
    D = D_MODEL
    f32 = jnp.float32

    def normal(shape, scale):
        return scale * jax.random.normal(next(keys), shape, f32)

    def gain(shape):
        return 1.0 + normal(shape, 0.05)

    n_idx = jnp.arange(S5_STATE, dtype=f32)
    g5, p5 = S5_GROUPS, S5_STATE
    return {
        'x_prompt': normal((BATCH, SEQ, D), 1.0),
        'x_sample': normal((DEC_BATCH, DEC_SEQ, D), 1.0),
        'c': normal((DEC_BATCH, D), 1.0),
        'cache_na_k': normal((DEC_BATCH, ne, PAST_LEN, NA_HEADS, HEAD_DIM), 1.0),
        'cache_na_v': normal((DEC_BATCH, ne, PAST_LEN, NA_HEADS, HEAD_DIM), 1.0),
        'state_s5': normal((DEC_BATCH, ne, 2, g5, p5, 2), 0.1),
        'cache_swa_k': normal((DEC_BATCH, no, PAST_LEN, SWA_KV_HEADS, HEAD_DIM), 1.0),
        'cache_swa_v': normal((DEC_BATCH, no, PAST_LEN, SWA_KV_HEADS, HEAD_DIM), 1.0),
        'c_ctx': normal((D,), 1.0),
        'norm_mix': gain((DEPTH, D)),
        'norm_ffn': gain((DEPTH, D)),
        'ada_w': normal((DEPTH, D, 6 * D), 0.5 * D ** -0.5),
        'ada_b': normal((DEPTH, 6 * D), 0.02),
        'ab_w_in': normal((ne, D, AB_IN), D ** -0.5),
        'ab_w_out': normal((ne, AB_OUT, D), AB_OUT ** -0.5),
        'na_rpb': normal((ne, NA_HEADS, 2 * NA_WIN_R - 1, 2 * NA_WIN_C - 1), 0.1),
        's5_a_re': -0.5 + normal((ne, 2, g5, p5), 0.01),
        's5_a_im': math.pi * n_idx + normal((ne, 2, g5, p5), 0.01),
        's5_log_dt': jax.random.uniform(next(keys), (ne, 2, g5), f32, math.log(1e-3), math.log(1e-1)),
        's5_b_re': normal((ne, 2, g5, p5, S5_GROUP), (2 * S5_GROUP) ** -0.5),
        's5_b_im': normal((ne, 2, g5, p5, S5_GROUP), (2 * S5_GROUP) ** -0.5),
        's5_c_re': normal((ne, 2, g5, S5_GROUP, p5), p5 ** -0.5),
        's5_c_im': normal((ne, 2, g5, S5_GROUP, p5), p5 ** -0.5),
        's5_d': normal((ne, S5_WIDTH), 1.0),
        's5_glu_w': normal((ne, S5_WIDTH, S5_WIDTH), S5_WIDTH ** -0.5),
        's5_glu_b': normal((ne, S5_WIDTH), 0.02),
        'swa_w_in': normal((no, D, SWA_IN), D ** -0.5),
        'swa_w_out': normal((no, SWA_Q, D), SWA_Q ** -0.5),
        'swa_sink': normal((no, SWA_HEADS), 0.5),
        'ffn_w_up': normal((DEPTH, D, 2 * D_FF), D ** -0.5),
        'ffn_conv_w': normal((DEPTH, CONV_W, 2 * D_FF), CONV_W ** -0.5),
        'ffn_conv_b': normal((DEPTH, 2 * D_FF), 0.02),
        'ffn_w_down': normal((DEPTH, D_FF, D), D_FF ** -0.5),
        'final_norm': gain((D,)),
    }


def _split_ab(p):
    bsz, l, _ = p.shape
    q = p[..., :NA_WIDTH].reshape(bsz, l, NA_HEADS, HEAD_DIM)
    k = p[..., NA_WIDTH:2 * NA_WIDTH].reshape(bsz, l, NA_HEADS, HEAD_DIM)
    v = p[..., 2 * NA_WIDTH:3 * NA_WIDTH].reshape(bsz, l, NA_HEADS, HEAD_DIM)
    u = p[..., 3 * NA_WIDTH:]
    return q, k, v, u


def _split_swa(p):
    bsz, l, _ = p.shape
    q = p[..., :SWA_Q].reshape(bsz, l, SWA_HEADS, HEAD_DIM)
    k = p[..., SWA_Q:SWA_Q + SWA_KV].reshape(bsz, l, SWA_KV_HEADS, HEAD_DIM)
    v = p[..., SWA_Q + SWA_KV:].reshape(bsz, l, SWA_KV_HEADS, HEAD_DIM)
    return q, k, v


def reference(x_prompt, x_sample, c, cache_na_k, cache_na_v, state_s5, cache_swa_k, cache_swa_v,
              c_ctx, norm_mix, norm_ffn, ada_w, ada_b, ab_w_in, ab_w_out, na_rpb,
              s5_a_re, s5_a_im, s5_log_dt, s5_b_re, s5_b_im, s5_c_re, s5_c_im, s5_d, s5_glu_w, s5_glu_b,
              swa_w_in, swa_w_out, swa_sink, ffn_w_up, ffn_conv_w, ffn_conv_b, ffn_w_down, final_norm):
    xp, xs = x_prompt, x_sample
    bp, lp, _ = xp.shape
    bs, ls, _ = xs.shape
    cond_p = c_ctx[None, :]
    na_k_list, na_v_list, s5_list, swa_k_list, swa_v_list = [], [], [], [], []
    for layer in range(DEPTH):
        i = layer // 2
        mp = adaln(cond_p, ada_w[layer], ada_b[layer])
        ms = adaln(c, ada_w[layer], ada_b[layer])
        hp = modulate(xp, norm_mix[layer], mp[0], mp[1])
        hs = modulate(xs, norm_mix[layer], ms[0], ms[1])
        if layer % 2 == 0:
            s5p = (s5_a_re[i], s5_a_im[i], s5_log_dt[i], s5_b_re[i], s5_b_im[i],
                   s5_c_re[i], s5_c_im[i], s5_d[i], s5_glu_w[i], s5_glu_b[i])
            q, k, v, u = _split_ab(hp @ ab_w_in[i])
            att = ctx_self_attention(q, k, v)
            h0 = jnp.zeros((bp, 2, S5_GROUPS, S5_STATE, 2), jnp.float32)
            s5o, s5_fin = s5_mixer(u, *s5p, h0, True)
            op = jnp.concatenate([att.reshape(bp, lp, NA_WIDTH), s5o], axis=-1) @ ab_w_out[i]
            na_k_list.append(k)
            na_v_list.append(v)
            s5_list.append(s5_fin)
            q, k, v, u = _split_ab(hs @ ab_w_in[i])
            att = neighbourhood_attention(q, k, v, cache_na_k[:, i], cache_na_v[:, i], na_rpb[i])
            s5o = s5_mixer(u, *s5p, state_s5[:, i], False)
            os_ = jnp.concatenate([att.reshape(bs, ls, NA_WIDTH), s5o], axis=-1) @ ab_w_out[i]
        else:
            q, k, v = _split_swa(hp @ swa_w_in[i])
            att = ctx_self_attention(q, k, v, swa_sink[i])
            op = att.reshape(bp, lp, SWA_Q) @ swa_w_out[i]
            swa_k_list.append(k)
            swa_v_list.append(v)
            q, k, v = _split_swa(hs @ swa_w_in[i])
            att = window_attention(axial_rope(q), axial_rope(k), v, cache_swa_k[:, i], cache_swa_v[:, i], swa_sink[i])
            os_ = att.reshape(bs, ls, SWA_Q) @ swa_w_out[i]
        xp = xp + mp[2] * op
        xs = xs + ms[2] * os_
        ffn_args = (ffn_w_up[layer], ffn_conv_w[layer], ffn_conv_b[layer], ffn_w_down[layer])
        xp = xp + mp[5] * conv_ffn(modulate(xp, norm_ffn[layer], mp[3], mp[4]), *ffn_args)
        xs = xs + ms[5] * conv_ffn(modulate(xs, norm_ffn[layer], ms[3], ms[4]), *ffn_args)
    y_prompt = rmsnorm(xp, final_norm)
    y_sample = rmsnorm(xs, final_norm)
    new_cache_na_k = jnp.stack(na_k_list, axis=1)
    new_cache_na_v = jnp.stack(na_v_list, axis=1)
    new_state_s5 = jnp.stack(s5_list, axis=1)
    new_cache_swa_k = jnp.stack(swa_k_list, axis=1)
    new_cache_swa_v = jnp.stack(swa_v_list, axis=1)
    return (y_prompt, y_sample, new_cache_na_k, new_cache_na_v, new_state_s5, new_cache_swa_k, new_cache_swa_v)
```

```python
import functools
import math

import jax
import jax.numpy as jnp
import numpy as np
from jax import lax
from jax.experimental import pallas as pl
from jax.experimental.pallas import tpu as pltpu

F32 = jnp.float32
BF16 = jnp.bfloat16

GRID_W = 64
HEAD_DIM = 128
NA_HEADS = 8
NA_WIN_R = 8
NA_WIN_C = 16
S5_GROUP = 16
S5_STATE = 64
SWA_HEADS = 16
SWA_KV_HEADS = 4
SWA_WINDOW = 128
ROPE_BASE = 10000.0
EPS = 1e-6
NEG_INF = -1e30
ATT_SCALE = HEAD_DIM ** -0.5

VMEM_PHYSICAL_MIB = 64
SUBLANES = 8
LANES = 128

S5_STEPS = 256
S5_ROWS = 16
S5_GB = 8
S5_GB_IN = S5_GB * S5_GROUP
S5_GB_ST = S5_GB * S5_STATE


def _params(sem, vmem_mib):
    assert vmem_mib <= VMEM_PHYSICAL_MIB
    if sem is None:
        return pltpu.CompilerParams(vmem_limit_bytes=vmem_mib << 20)
    return pltpu.CompilerParams(dimension_semantics=sem, vmem_limit_bytes=vmem_mib << 20)


def _dot(a, b):
    return jnp.dot(a, b, preferred_element_type=F32)


def _dot_t(a, b):
    return lax.dot_general(a, b, (((1,), (1,)), ((), ())), preferred_element_type=F32)


def _sigmoid(x):
    return 1.0 / (1.0 + jnp.exp(-x))


def _adaln_kernel(c_ref, w_ref, b_ref, o_ref):
    c = c_ref[...]
    s = (c * _sigmoid(c)).astype(BF16)
    o_ref[0] = _dot(s, w_ref[0].astype(BF16)) + b_ref[0]


def _adaln(cond8, ada_w, ada_b):
    depth, d, n = ada_w.shape
    tn = 1024
    return pl.pallas_call(
        _adaln_kernel,
        grid=(depth, n // tn),
        in_specs=[
            pl.BlockSpec((SUBLANES, d), lambda l, j: (0, 0)),
            pl.BlockSpec((1, d, tn), lambda l, j: (l, 0, j)),
            pl.BlockSpec((1, 1, tn), lambda l, j: (l, 0, j)),
        ],
        out_specs=pl.BlockSpec((1, SUBLANES, tn), lambda l, j: (l, 0, j)),
        out_shape=jax.ShapeDtypeStruct((depth, SUBLANES, n), F32),
        compiler_params=_params(("arbitrary", "arbitrary"), 40),
        name="adaln",
    )(cond8, ada_w, ada_b.reshape(depth, 1, n))


def _group_of(row0, n_ctx, lat_len):
    return jnp.maximum((row0 - n_ctx) // lat_len + 1, 0)


def _modulated(x, g, shift, scale):
    ms = jnp.mean(x * x, axis=-1, keepdims=True)
    y = x * lax.rsqrt(ms + EPS) * g
    return y * (1.0 + scale) + shift


def _normproj_kernel(x_ref, sh_ref, sc_ref, g_ref, w_ref, o_ref, h_ref):
    @pl.when(pl.program_id(1) == 0)
    def _():
        h = _modulated(x_ref[...], g_ref[...], sh_ref[0, 0], sc_ref[0, 0])
        h_ref[...] = h.astype(BF16)

    o_ref[...] = _dot(h_ref[...], w_ref[...])


def _normproj(x, mod, which, g, w, n_ctx, lat_len, tm=1024, tn=512):
    t, d = x.shape
    n = w.shape[1]
    grp = lambda i: _group_of(i * tm, n_ctx, lat_len)
    return pl.pallas_call(
        _normproj_kernel,
        grid=(t // tm, n // tn),
        in_specs=[
            pl.BlockSpec((tm, d), lambda i, j: (i, 0)),
            pl.BlockSpec((1, 1, 1, d), lambda i, j: (grp(i), which, 0, 0)),
            pl.BlockSpec((1, 1, 1, d), lambda i, j: (grp(i), which + 1, 0, 0)),
            pl.BlockSpec((1, d), lambda i, j: (0, 0)),
            pl.BlockSpec((d, tn), lambda i, j: (0, j)),
        ],
        out_specs=pl.BlockSpec((tm, tn), lambda i, j: (i, j)),
        out_shape=jax.ShapeDtypeStruct((t, n), F32),
        scratch_shapes=[pltpu.VMEM((tm, d), BF16)],
        compiler_params=_params(("arbitrary", "arbitrary"), 48),
        name="normproj",
    )(x, mod, mod, g.reshape(1, d), w)


def _outproj_kernel(*refs, n_parts):
    lhs = refs[:n_parts]
    ws = refs[n_parts:2 * n_parts]
    x_ref, gate_ref, o_ref = refs[2 * n_parts:]
    acc = _dot(lhs[0][...], ws[0][...])
    for p in range(1, n_parts):
        acc = acc + _dot(lhs[p][...], ws[p][...])
    o_ref[...] = x_ref[...] + gate_ref[0, 0] * acc


def _outproj(parts, w, x, mod, which, n_ctx, lat_len, tm=512, tn=512):
    t, d = x.shape
    n_parts = len(parts)
    grp = lambda i: _group_of(i * tm, n_ctx, lat_len)
    in_specs, args = [], []
    for p in parts:
        in_specs.append(pl.BlockSpec((tm, p.shape[1]), lambda i, j: (i, 0)))
        args.append(p)
    row = 0
    for idx, p in enumerate(parts):
        k = p.shape[1]
        assert row % k == 0
        in_specs.append(pl.BlockSpec((k, tn), functools.partial(lambda i, j, rb: (rb, j), rb=row // k)))
        args.append(w)
        row += k
    assert row == w.shape[0]
    in_specs += [
        pl.BlockSpec((tm, tn), lambda i, j: (i, j)),
        pl.BlockSpec((1, 1, 1, tn), lambda i, j: (grp(i), which, 0, j)),
    ]
    args += [x, mod]
    return pl.pallas_call(
        functools.partial(_outproj_kernel, n_parts=n_parts),
        grid=(t // tm, d // tn),
        in_specs=in_specs,
        out_specs=pl.BlockSpec((tm, tn), lambda i, j: (i, j)),
        out_shape=jax.ShapeDtypeStruct((t, d), F32),
        compiler_params=_params(("arbitrary", "arbitrary"), 48),
        name="outproj",
    )(*args)


HALO = SUBLANES


def _ffn_up_kernel(x_ref, xp_ref, xn_ref, sh_ref, sc_ref, g_ref, wg_ref, wu_ref, cg_ref, cu_ref,
                   bg_ref, bu_ref, o_ref, h_ref, *, tm, n_ctx, ctx_len, lat_len):
    i = pl.program_id(0)
    row0 = i * tm
    is_ctx = row0 < n_ctx

    assert ctx_len & (ctx_len - 1) == 0 and lat_len & (lat_len - 1) == 0 and n_ctx % lat_len == 0

    def seq_pos(r):
        return jnp.where(is_ctx, r & (ctx_len - 1), r & (lat_len - 1))

    def seq_len():
        return jnp.where(is_ctx, ctx_len, lat_len)

    @pl.when(pl.program_id(1) == 0)
    def _():
        g = g_ref[...]
        sh = sh_ref[0, 0]
        sc = sc_ref[0, 0]
        h_ref[HALO:HALO + tm, :] = _modulated(x_ref[...], g, sh, sc).astype(BF16)
        keep_prev = (seq_pos(row0) != 0).astype(F32)
        keep_next = (seq_pos(row0 + tm - 1) != seq_len() - 1).astype(F32)
        h_ref[0:HALO, :] = (_modulated(xp_ref[...], g, sh, sc) * keep_prev).astype(BF16)
        h_ref[HALO + tm:, :] = (_modulated(xn_ref[...], g, sh, sc) * keep_next).astype(BF16)

    rows = row0 + lax.broadcasted_iota(jnp.int32, (tm, 1), 0)
    pos = seq_pos(rows)
    keep_p = (pos != 0).astype(F32)
    keep_n = (pos != seq_len() - 1).astype(F32)
    tot = tm + 2 * HALO

    def conv(w_ref, c_ref, b_ref):
        z = _dot(h_ref[...], w_ref[...])
        zp = pltpu.roll(z, 1, axis=0)[HALO:HALO + tm]
        zn = pltpu.roll(z, tot - 1, axis=0)[HALO:HALO + tm]
        c = c_ref[...]
        return (zp * keep_p) * c[0:1] + z[HALO:HALO + tm] * c[1:2] + (zn * keep_n) * c[2:3] + b_ref[...]

    gate = conv(wg_ref, cg_ref, bg_ref)
    up = conv(wu_ref, cu_ref, bu_ref)
    o_ref[...] = (gate * _sigmoid(gate) * up).astype(BF16)


def _ffn_up(x, mod, g, w_up, conv_w, conv_b, n_ctx, ctx_len, lat_len, tm=1024, tn=512):
    t, d = x.shape
    f = w_up.shape[1] // 2
    nj = f // tn
    grp = lambda i: _group_of(i * tm, n_ctx, lat_len)
    hb = tm // HALO
    last_blk = t // HALO - 1
    kern = functools.partial(_ffn_up_kernel, tm=tm, n_ctx=n_ctx, ctx_len=ctx_len, lat_len=lat_len)
    return pl.pallas_call(
        kern,
        grid=(t // tm, nj),
        in_specs=[
            pl.BlockSpec((tm, d), lambda i, j: (i, 0)),
            pl.BlockSpec((HALO, d), lambda i, j: (jnp.maximum(i * hb - 1, 0), 0)),
            pl.BlockSpec((HALO, d), lambda i, j: (jnp.minimum((i + 1) * hb, last_blk), 0)),
            pl.BlockSpec((1, 1, 1, d), lambda i, j: (grp(i), 3, 0, 0)),
            pl.BlockSpec((1, 1, 1, d), lambda i, j: (grp(i), 4, 0, 0)),
            pl.BlockSpec((1, d), lambda i, j: (0, 0)),
            pl.BlockSpec((d, tn), lambda i, j: (0, j)),
            pl.BlockSpec((d, tn), lambda i, j: (0, j + nj)),
            pl.BlockSpec((3, tn), lambda i, j: (0, j)),
            pl.BlockSpec((3, tn), lambda i, j: (0, j + nj)),
            pl.BlockSpec((1, tn), lambda i, j: (0, j)),
            pl.BlockSpec((1, tn), lambda i, j: (0, j + nj)),
        ],
        out_specs=pl.BlockSpec((tm, tn), lambda i, j: (i, j)),
        out_shape=jax.ShapeDtypeStruct((t, f), BF16),
        scratch_shapes=[pltpu.VMEM((tm + 2 * HALO, d), BF16)],
        compiler_params=_params(("arbitrary", "arbitrary"), 56),
        name="ffn_up",
    )(x, x, x, mod, mod, g.reshape(1, d), w_up, w_up, conv_w, conv_w,
      conv_b.reshape(1, 2 * f), conv_b.reshape(1, 2 * f))


def _rmsnorm_kernel(x_ref, g_ref, o_ref):
    x = x_ref[...]
    ms = jnp.mean(x * x, axis=-1, keepdims=True)
    o_ref[...] = x * lax.rsqrt(ms + EPS) * g_ref[...]


def _rmsnorm(x, g, tm=1024):
    t, d = x.shape
    return pl.pallas_call(
        _rmsnorm_kernel,
        grid=(t // tm,),
        in_specs=[pl.BlockSpec((tm, d), lambda i: (i, 0)), pl.BlockSpec((1, d), lambda i: (0, 0))],
        out_specs=pl.BlockSpec((tm, d), lambda i: (i, 0)),
        out_shape=jax.ShapeDtypeStruct((t, d), F32),
        compiler_params=_params(("arbitrary",), 48),
        name="final_norm",
    )(x, g.reshape(1, d))


def _attend(score_blocks, value_blocks, sink=None):
    m = jnp.max(score_blocks[0], axis=-1, keepdims=True)
    for s in score_blocks[1:]:
        m = jnp.maximum(m, jnp.max(s, axis=-1, keepdims=True))
    if sink is not None:
        m = jnp.maximum(m, sink)
    den = jnp.exp(sink - m) if sink is not None else 0.0
    out = None
    for s, v in zip(score_blocks, value_blocks):
        p = jnp.exp(s - m)
        den = den + jnp.sum(p, axis=-1, keepdims=True)
        o = _dot(p.astype(BF16), v)
        out = o if out is None else out + o
    return out / den


def _ctx_attn_kernel(*refs, n_heads, n_kv, has_sink):
    if has_sink:
        sink_ref, q_ref, k_ref, v_ref, o_ref = refs
    else:
        q_ref, k_ref, v_ref, o_ref = refs
    grp = n_heads // n_kv
    for kv in range(n_kv):
        ksl = slice(kv * HEAD_DIM, (kv + 1) * HEAD_DIM)
        k = k_ref[:, ksl].astype(BF16)
        v = v_ref[:, ksl].astype(BF16)
        for gi in range(grp):
            h = kv * grp + gi
            hsl = slice(h * HEAD_DIM, (h + 1) * HEAD_DIM)
            q = q_ref[:, hsl].astype(BF16)
            s = _dot_t(q, k) * ATT_SCALE
            sink = sink_ref[h] if has_sink else None
            o_ref[:, hsl] = _attend([s], [v], sink).astype(BF16)


def _ctx_attn(proj, n_seq, seq_len, n_heads, n_kv, sink=None):
    qw = n_heads * HEAD_DIM
    kw = n_kv * HEAD_DIM
    assert qw % kw == 0
    kb = qw // kw
    has_sink = sink is not None
    in_specs = [
        pl.BlockSpec((seq_len, qw), lambda b: (b, 0)),
        pl.BlockSpec((seq_len, kw), lambda b: (b, kb)),
        pl.BlockSpec((seq_len, kw), lambda b: (b, kb + 1)),
    ]
    args = [proj, proj, proj]
    if has_sink:
        in_specs = [pl.BlockSpec(memory_space=pltpu.SMEM)] + in_specs
        args = [sink] + args
    return pl.pallas_call(
        functools.partial(_ctx_attn_kernel, n_heads=n_heads, n_kv=n_kv, has_sink=has_sink),
        grid=(n_seq,),
        in_specs=in_specs,
        out_specs=pl.BlockSpec((seq_len, qw), lambda b: (b, 0)),
        out_shape=jax.ShapeDtypeStruct((n_seq * seq_len, qw), BF16),
        compiler_params=_params(("arbitrary",), 32),
        name="ctx_attn",
    )(*args)


NA_QROWS = 8
NA_KROWS = 16
NA_QBLK = NA_QROWS * GRID_W
NA_KBLK = NA_KROWS * GRID_W


def _na_key_row0(blk, rows):
    return np.clip(blk * NA_QROWS - NA_WIN_R // 2, 0, rows - NA_KROWS)


def _na_bias_tables(rpb, rows):
    h = rpb.shape[0]
    nblk = rows // NA_QROWS
    col = np.arange(GRID_W)
    col_start = np.clip(col - NA_WIN_C // 2, 0, GRID_W - NA_WIN_C)
    col_ok = (col[None, :] >= col_start[:, None]) & (col[None, :] < col_start[:, None] + NA_WIN_C)
    dc = np.clip(col[None, :] - col[:, None] + NA_WIN_C - 1, 0, 2 * NA_WIN_C - 2)
    tiles = jnp.take(rpb.astype(F32), jnp.asarray(dc.reshape(-1)), axis=2).reshape(
        h, 2 * NA_WIN_R - 1, GRID_W, GRID_W)
    tiles = jnp.where(jnp.asarray(col_ok)[None, None], tiles, NEG_INF)
    tiles = jnp.concatenate([tiles, jnp.full((h, 1, GRID_W, GRID_W), NEG_INF, F32)], axis=1)
    masked = 2 * NA_WIN_R - 1
    a_idx = np.full((3, NA_QROWS, NA_KROWS), masked, np.int32)
    for v, blk in enumerate((0, 1, nblk - 1)):
        kr0 = _na_key_row0(blk, rows)
        for ql in range(NA_QROWS):
            qr = blk * NA_QROWS + ql
            start = np.clip(qr - NA_WIN_R // 2, 0, rows - NA_WIN_R)
            for kl in range(NA_KROWS):
                kr = kr0 + kl
                if start <= kr < start + NA_WIN_R:
                    a_idx[v, ql, kl] = kr - qr + NA_WIN_R - 1
    t = jnp.take(tiles, jnp.asarray(a_idx.reshape(-1)), axis=1)
    t = t.reshape(h, 3, NA_QROWS, NA_KROWS, GRID_W, GRID_W).transpose(1, 0, 2, 4, 3, 5)
    return t.reshape(3, h, NA_QBLK, NA_KBLK)


def _na_kernel(q_ref, k_ref, v_ref, kc_ref, vc_ref, bias_ref, o_ref, *, rows):
    i = pl.program_id(2)
    kr0 = jnp.clip(i * NA_QROWS - NA_WIN_R // 2, 0, rows - NA_KROWS)
    k0 = pl.multiple_of(kr0 * GRID_W, GRID_W * (NA_WIN_R // 2))
    q = q_ref[...].astype(BF16)
    kw = k_ref[pl.ds(k0, NA_KBLK), :].astype(BF16)
    vw = v_ref[pl.ds(k0, NA_KBLK), :].astype(BF16)
    s_loc = _dot_t(q, kw) * ATT_SCALE + bias_ref[0, 0]
    s_ctx = _dot_t(q, kc_ref[0].astype(BF16)) * ATT_SCALE
    o_ref[...] = _attend([s_loc, s_ctx], [vw, vc_ref[0].astype(BF16)]).astype(BF16)


def _na_attn(proj, cache_k, cache_v, bias, n_ctx, n_lat, lat_len):
    rows = lat_len // GRID_W
    nblk = lat_len // NA_QBLK
    assert n_ctx % lat_len == 0 and n_ctx % NA_QBLK == 0
    qb0 = n_ctx // NA_QBLK
    sb0 = n_ctx // lat_len
    h = NA_HEADS
    c = cache_k.shape[1]
    variant = lambda i: jnp.where(i == 0, 0, jnp.where(i == nblk - 1, 2, 1))
    return pl.pallas_call(
        functools.partial(_na_kernel, rows=rows),
        grid=(n_lat, h, nblk),
        in_specs=[
            pl.BlockSpec((NA_QBLK, HEAD_DIM), lambda b, hh, i: (qb0 + b * nblk + i, hh)),
            pl.BlockSpec((lat_len, HEAD_DIM), lambda b, hh, i: (sb0 + b, h + hh)),
            pl.BlockSpec((lat_len, HEAD_DIM), lambda b, hh, i: (sb0 + b, 2 * h + hh)),
            pl.BlockSpec((1, c, HEAD_DIM), lambda b, hh, i: (b, 0, hh)),
            pl.BlockSpec((1, c, HEAD_DIM), lambda b, hh, i: (b, 0, hh)),
            pl.BlockSpec((1, 1, NA_QBLK, NA_KBLK), lambda b, hh, i: (variant(i), hh, 0, 0)),
        ],
        out_specs=pl.BlockSpec((NA_QBLK, HEAD_DIM), lambda b, hh, i: (b * nblk + i, hh)),
        out_shape=jax.ShapeDtypeStruct((n_lat * lat_len, h * HEAD_DIM), BF16),
        compiler_params=_params(("arbitrary", "arbitrary", "arbitrary"), 48),
        name="na_attn",
    )(proj, proj, proj, cache_k, cache_v, bias)


SWA_QBLK = 512
SWA_KBLK = SWA_QBLK + 2 * SWA_WINDOW


def _rope_tables(length):
    half = HEAD_DIM // 2
    freqs = ROPE_BASE ** (-jnp.arange(0, half, 2, dtype=F32) / half)
    t = jnp.arange(length)
    ang_r = (t // GRID_W).astype(F32)[:, None] * freqs[None, :]
    ang_c = (t % GRID_W).astype(F32)[:, None] * freqs[None, :]
    ang = jnp.concatenate([ang_r, ang_r, ang_c, ang_c], axis=1)
    sign = np.concatenate([-np.ones(half // 2), np.ones(half // 2)] * 2).astype(np.float32)
    return jnp.cos(ang), jnp.sin(ang) * jnp.asarray(sign)[None, :]


def _rope(x, cos, sin_signed):
    lane = lax.broadcasted_iota(jnp.int32, x.shape, 1)
    quarter = HEAD_DIM // 4
    lower = (lane & (2 * quarter - 1)) < quarter
    swapped = jnp.where(lower, pltpu.roll(x, HEAD_DIM - quarter, axis=1), pltpu.roll(x, quarter, axis=1))
    return x * cos + swapped * sin_signed


def _swa_kernel(sink_ref, q_ref, k_ref, v_ref, kc_ref, vc_ref, cos_ref, sin_ref, o_ref, *, lat_len):
    kv = pl.program_id(1)
    i = pl.program_id(2)
    grp = SWA_HEADS // SWA_KV_HEADS
    q0 = pl.multiple_of(i * SWA_QBLK, SWA_QBLK)
    k0 = pl.multiple_of(jnp.clip(i * SWA_QBLK - SWA_WINDOW, 0, lat_len - SWA_KBLK), SWA_WINDOW)
    kw = _rope(k_ref[pl.ds(k0, SWA_KBLK), :], cos_ref[pl.ds(k0, SWA_KBLK), :],
               sin_ref[pl.ds(k0, SWA_KBLK), :]).astype(BF16)
    vw = v_ref[pl.ds(k0, SWA_KBLK), :].astype(BF16)
    kc = kc_ref[0].astype(BF16)
    vc = vc_ref[0].astype(BF16)
    qpos = q0 + lax.broadcasted_iota(jnp.int32, (SWA_QBLK, SWA_KBLK), 0)
    kpos = k0 + lax.broadcasted_iota(jnp.int32, (SWA_QBLK, SWA_KBLK), 1)
    band = jnp.where(jnp.abs(qpos - kpos) <= SWA_WINDOW, 0.0, NEG_INF)
    cos_q = cos_ref[pl.ds(q0, SWA_QBLK), :]
    sin_q = sin_ref[pl.ds(q0, SWA_QBLK), :]
    for gi in range(grp):
        hsl = slice(gi * HEAD_DIM, (gi + 1) * HEAD_DIM)
        q = _rope(q_ref[:, hsl], cos_q, sin_q).astype(BF16)
        s_loc = _dot_t(q, kw) * ATT_SCALE + band
        s_ctx = _dot_t(q, kc) * ATT_SCALE
        sink = sink_ref[kv * grp + gi]
        o_ref[:, hsl] = _attend([s_loc, s_ctx], [vw, vc], sink).astype(BF16)


def _swa_attn(proj, cache_k, cache_v, sink, cos, sin_signed, n_ctx, n_lat, lat_len):
    grp = SWA_HEADS // SWA_KV_HEADS
    gw = grp * HEAD_DIM
    nblk = lat_len // SWA_QBLK
    qb0 = n_ctx // SWA_QBLK
    sb0 = n_ctx // lat_len
    kcol0 = SWA_HEADS
    vcol0 = SWA_HEADS + SWA_KV_HEADS
    c = cache_k.shape[1]
    return pl.pallas_call(
        functools.partial(_swa_kernel, lat_len=lat_len),
        grid=(n_lat, SWA_KV_HEADS, nblk),
        in_specs=[
            pl.BlockSpec(memory_space=pltpu.SMEM),
            pl.BlockSpec((SWA_QBLK, gw), lambda b, kv, i: (qb0 + b * nblk + i, kv)),
            pl.BlockSpec((lat_len, HEAD_DIM), lambda b, kv, i: (sb0 + b, kcol0 + kv)),
            pl.BlockSpec((lat_len, HEAD_DIM), lambda b, kv, i: (sb0 + b, vcol0 + kv)),
            pl.BlockSpec((1, c, HEAD_DIM), lambda b, kv, i: (b, 0, kv)),
            pl.BlockSpec((1, c, HEAD_DIM), lambda b, kv, i: (b, 0, kv)),
            pl.BlockSpec((lat_len, HEAD_DIM), lambda b, kv, i: (0, 0)),
            pl.BlockSpec((lat_len, HEAD_DIM), lambda b, kv, i: (0, 0)),
        ],
        out_specs=pl.BlockSpec((SWA_QBLK, gw), lambda b, kv, i: (b * nblk + i, kv)),
        out_shape=jax.ShapeDtypeStruct((n_lat * lat_len, SWA_HEADS * HEAD_DIM), BF16),
        compiler_params=_params(("arbitrary", "arbitrary", "arbitrary"), 48),
        name="swa_attn",
    )(sink, proj, proj, proj, cache_k, cache_v, cos, sin_signed)


def _s5_disc_kernel(are_ref, aim_ref, ldt_ref, bre_ref, bim_ref, abre_ref, abim_ref, bbre_ref, bbim_ref):
    a_re = are_ref[...]
    a_im = aim_ref[...]
    dt = jnp.exp(ldt_ref[...])
    mag = jnp.exp(a_re * dt)
    ang = a_im * dt
    ab_re = mag * jnp.cos(ang)
    ab_im = mag * jnp.sin(ang)
    den = a_re * a_re + a_im * a_im
    n_re = ab_re - 1.0
    f_re = (n_re * a_re + ab_im * a_im) / den
    f_im = (ab_im * a_re - n_re * a_im) / den
    b_re = bre_ref[...]
    b_im = bim_ref[...]
    abre_ref[...] = ab_re
    abim_ref[...] = ab_im
    bbre_ref[...] = f_re * b_re - f_im * b_im
    bbim_ref[...] = f_re * b_im + f_im * b_re


def _s5_discretise(a_re, a_im, log_dt, b_re, b_im):
    nd, g, p, c = b_re.shape
    shape2 = (nd * g, p * c)
    expand = lambda a: jnp.broadcast_to(a[..., None], (nd, g, p, c)).reshape(shape2)
    ldt = jnp.broadcast_to(log_dt[:, :, None, None], (nd, g, p, c)).reshape(shape2)
    out = pl.pallas_call(
        _s5_disc_kernel,
        out_shape=[jax.ShapeDtypeStruct(shape2, F32)] * 4,
        compiler_params=_params(None, 16),
        name="s5_discretise",
    )(expand(a_re), expand(a_im), ldt, b_re.reshape(shape2), b_im.reshape(shape2))
    ab_re, ab_im, bb_re, bb_im = [o.reshape(nd, g, p, c) for o in out]
    return ab_re[..., 0], ab_im[..., 0], bb_re, bb_im


def _block_diag(m, gb):
    g, a, b = m.shape
    m = m.reshape(g // gb, gb, a, b)
    eye = jnp.eye(gb, dtype=m.dtype)
    out = m[:, :, :, None, :] * eye[None, :, None, :, None]
    return out.reshape(g // gb, gb * a, gb * b)


def _cmul_add(ar, ai, xr, xi, br, bi):
    return ar * xr - ai * xi + br, ar * xi + ai * xr + bi


def _s5_kernel(*refs, latent):
    if latent:
        (u_ref, bbr_ref, bbi_ref, ccr_ref, cci_ref, ar_ref, ai_ref, d_ref, h0_ref,
         y_ref, xr_s, xi_s, cr_s, ci_s) = refs
    else:
        (u_ref, bbr_ref, bbi_ref, ccr_ref, cci_ref, ar_ref, ai_ref, d_ref,
         y_ref, fin_ref, xr_s, xi_s) = refs
    rows, steps = S5_ROWS, S5_STEPS
    u = u_ref[0]
    ub = u.astype(BF16)
    y = d_ref[0] * u
    for dirn in range(2):
        reverse = dirn == 1
        xr_s[...] = _dot(ub, bbr_ref[dirn, 0])
        xi_s[...] = _dot(ub, bbi_ref[dirn, 0])
        a_r1 = ar_ref[dirn, 0]
        a_i1 = ai_ref[dirn, 0]
        a_r = jnp.broadcast_to(a_r1, (rows, S5_GB_ST))
        a_i = jnp.broadcast_to(a_i1, (rows, S5_GB_ST))

        def time_rows(s):
            t = (steps - 1 - s) if reverse else s
            if isinstance(t, int):
                return pl.ds(t * rows, rows)
            return pl.ds(pl.multiple_of(t * rows, rows), rows)

        first = time_rows(0)
        final = time_rows(steps - 1)

        def scan_body(s, carry):
            pr, pi = carry
            sl = time_rows(s)
            nr, ni = _cmul_add(a_r, a_i, pr, pi, xr_s[sl, :], xi_s[sl, :])
            xr_s[sl, :] = nr
            xi_s[sl, :] = ni
            return nr, ni

        lax.fori_loop(1, steps, scan_body, (xr_s[first, :], xi_s[first, :]), unroll=4)

        if not latent:
            fin_ref[0, dirn, 0] = xr_s[final, :]
            fin_ref[0, dirn, 1] = xi_s[final, :]
        else:
            p_r, p_i = a_r1, a_i1
            for _ in range(int(math.log2(steps))):
                p_r, p_i = p_r * p_r - p_i * p_i, 2.0 * p_r * p_i
            assert 1 << int(math.log2(steps)) == steps
            e_r = xr_s[final, :]
            e_i = xi_s[final, :]
            c_r = h0_ref[0, 2 * dirn]
            c_i = h0_ref[0, 2 * dirn + 1]
            order = range(rows - 1, -1, -1) if reverse else range(rows)
            for j in order:
                cr_s[j:j + 1, :] = c_r
                ci_s[j:j + 1, :] = c_i
                c_r, c_i = _cmul_add(p_r, p_i, c_r, c_i, e_r[j:j + 1], e_i[j:j + 1])
            zero = jnp.zeros((rows, S5_GB_ST), F32)
            d_r, d_i = _cmul_add(a_r, a_i, cr_s[...], ci_s[...], zero, zero)

            def fix_body(s, carry):
                dr, di = carry
                sl = time_rows(s)
                xr_s[sl, :] = xr_s[sl, :] + dr
                xi_s[sl, :] = xi_s[sl, :] + di
                return _cmul_add(a_r, a_i, dr, di, zero, zero)

            lax.fori_loop(0, steps, fix_body, (d_r, d_i), unroll=4)

        y = y + _dot(xr_s[...].astype(BF16), ccr_ref[dirn, 0]) - _dot(xi_s[...].astype(BF16), cci_ref[dirn, 0])
    y_ref[0] = y


def _s5_scan(u_tm, bbr, bbi, ccr, cci, ab_re, ab_im, d, h0=None):
    tiles, tr, w = u_tm.shape
    assert tr == S5_STEPS * S5_ROWS
    nb = w // S5_GB_IN
    latent = h0 is not None
    in_specs = [
        pl.BlockSpec((1, tr, S5_GB_IN), lambda t, gb: (t, 0, gb)),
        pl.BlockSpec((2, 1, S5_GB_IN, S5_GB_ST), lambda t, gb: (0, gb, 0, 0)),
        pl.BlockSpec((2, 1, S5_GB_IN, S5_GB_ST), lambda t, gb: (0, gb, 0, 0)),
        pl.BlockSpec((2, 1, S5_GB_ST, S5_GB_IN), lambda t, gb: (0, gb, 0, 0)),
        pl.BlockSpec((2, 1, S5_GB_ST, S5_GB_IN), lambda t, gb: (0, gb, 0, 0)),
        pl.BlockSpec((2, 1, 1, S5_GB_ST), lambda t, gb: (0, gb, 0, 0)),
        pl.BlockSpec((2, 1, 1, S5_GB_ST), lambda t, gb: (0, gb, 0, 0)),
        pl.BlockSpec((1, 1, S5_GB_IN), lambda t, gb: (gb, 0, 0)),
    ]
    args = [u_tm, bbr, bbi, ccr, cci, ab_re, ab_im, d]
    y_spec = pl.BlockSpec((1, tr, S5_GB_IN), lambda t, gb: (t, 0, gb))
    y_shape = jax.ShapeDtypeStruct((tiles, tr, w), F32)
    scratch = [pltpu.VMEM((tr, S5_GB_ST), F32), pltpu.VMEM((tr, S5_GB_ST), F32)]
    if latent:
        in_specs.append(pl.BlockSpec((1, 4, 1, S5_GB_ST), lambda t, gb: (t, 0, 0, gb)))
        args.append(h0)
        out_specs, out_shape = y_spec, y_shape
        scratch += [pltpu.VMEM((S5_ROWS, S5_GB_ST), F32), pltpu.VMEM((S5_ROWS, S5_GB_ST), F32)]
    else:
        out_specs = [y_spec, pl.BlockSpec((1, 2, 2, S5_ROWS, S5_GB_ST), lambda t, gb: (t, 0, 0, 0, gb))]
        out_shape = [y_shape, jax.ShapeDtypeStruct((tiles, 2, 2, S5_ROWS, nb * S5_GB_ST), F32)]
    return pl.pallas_call(
        functools.partial(_s5_kernel, latent=latent),
        grid=(tiles, nb),
        in_specs=in_specs,
        out_specs=out_specs,
        out_shape=out_shape,
        scratch_shapes=scratch,
        compiler_params=_params(("arbitrary", "arbitrary"), 48),
        name="s5_scan_lat" if latent else "s5_scan_ctx",
    )(*args)


def _glu_kernel(y_ref, w_ref, b_ref, o_ref):
    y = y_ref[...]
    g = 0.5 * y * (1.0 + jnp.tanh(math.sqrt(2.0 / math.pi) * (y + 0.044715 * (y * y * y))))
    z = _dot(g.astype(BF16), w_ref[...]) + b_ref[...]
    o_ref[...] = (g * _sigmoid(z)).astype(BF16)


def _glu(y, w, b, tm=1024):
    t, n = y.shape
    return pl.pallas_call(
        _glu_kernel,
        grid=(t // tm,),
        in_specs=[
            pl.BlockSpec((tm, n), lambda i: (i, 0)),
            pl.BlockSpec((n, n), lambda i: (0, 0)),
            pl.BlockSpec((1, n), lambda i: (0, 0)),
        ],
        out_specs=pl.BlockSpec((tm, n), lambda i: (i, 0)),
        out_shape=jax.ShapeDtypeStruct((t, n), BF16),
        compiler_params=_params(("arbitrary",), 32),
        name="s5_glu",
    )(y, w, b.reshape(1, n))


def _s5_mixer(u, p, state, n_ctx_seq, ctx_len, n_lat, lat_len):
    t, w = u.shape
    g = w // S5_GROUP
    assert ctx_len == S5_STEPS and lat_len == S5_STEPS * S5_ROWS and n_ctx_seq % S5_ROWS == 0
    ab_re, ab_im, bb_re, bb_im = _s5_discretise(p["a_re"], p["a_im"], p["log_dt"], p["b_re"], p["b_im"])
    nb = g // S5_GB
    to_in = lambda m: jnp.stack([_block_diag(m[d].transpose(0, 2, 1), S5_GB) for d in range(2)]).astype(BF16)
    to_out = lambda m: jnp.stack([_block_diag(m[d].transpose(0, 2, 1), S5_GB) for d in range(2)]).astype(BF16)
    bbr, bbi = to_in(bb_re), to_in(bb_im)
    ccr, cci = to_out(p["c_re"].astype(F32)), to_out(p["c_im"].astype(F32))
    a_r = ab_re.reshape(2, nb, 1, S5_GB_ST)
    a_i = ab_im.reshape(2, nb, 1, S5_GB_ST)
    d = p["d"].astype(F32).reshape(nb, 1, S5_GB_IN)
    tiles = t // (S5_ROWS * S5_STEPS)
    ctx_tiles = n_ctx_seq // S5_ROWS
    u_tm = u.reshape(tiles, S5_ROWS, S5_STEPS, w).transpose(0, 2, 1, 3).reshape(tiles, S5_STEPS * S5_ROWS, w)
    y_ctx, fin = _s5_scan(u_tm[:ctx_tiles], bbr, bbi, ccr, cci, a_r, a_i, d)
    h0 = state.astype(F32).transpose(0, 1, 4, 2, 3).reshape(n_lat, 4, 1, g * S5_STATE)
    y_lat = _s5_scan(u_tm[ctx_tiles:], bbr, bbi, ccr, cci, a_r, a_i, d, h0=h0)
    y = jnp.concatenate([y_ctx, y_lat], axis=0).reshape(t, w)
    o = _glu(y, p["glu_w"].astype(BF16), p["glu_b"].astype(F32))
    o = o.reshape(tiles, S5_STEPS, S5_ROWS, w).transpose(0, 2, 1, 3).reshape(t, w)
    fin = fin.transpose(0, 3, 1, 4, 2).reshape(n_ctx_seq, 2, g, S5_STATE, 2)
    return o, fin


def kernel(x_prompt, x_sample, c, cache_na_k, cache_na_v, state_s5, cache_swa_k, cache_swa_v, c_ctx, norm_mix, norm_ffn, ada_w, ada_b, ab_w_in, ab_w_out, na_rpb, s5_a_re, s5_a_im, s5_log_dt, s5_b_re, s5_b_im, s5_c_re, s5_c_im, s5_d, s5_glu_w, s5_glu_b, swa_w_in, swa_w_out, swa_sink, ffn_w_up, ffn_conv_w, ffn_conv_b, ffn_w_down, final_norm):
    bp, lp, d = x_prompt.shape
    bs, ls, _ = x_sample.shape
    depth = ada_w.shape[0]
    n_ctx = bp * lp
    n_lat = bs * ls
    na_w = NA_HEADS * HEAD_DIM
    swa_q = SWA_HEADS * HEAD_DIM
    swa_kv = SWA_KV_HEADS * HEAD_DIM

    x = jnp.concatenate([x_prompt.reshape(n_ctx, d), x_sample.reshape(n_lat, d)], axis=0)
    cond = jnp.concatenate([c_ctx[None, :], c, jnp.zeros((SUBLANES - 1 - bs, d), F32)], axis=0)
    mods = _adaln(cond, ada_w, ada_b).reshape(depth, SUBLANES, 6, 1, d)
    cos, sin_signed = _rope_tables(ls)

    na_k, na_v, s5_fin, swa_k, swa_v = [], [], [], [], []
    for layer in range(depth):
        i = layer // 2
        mod = mods[layer]
        if layer % 2 == 0:
            proj = _normproj(x, mod, 0, norm_mix[layer], ab_w_in[i].astype(BF16), n_ctx, ls)
            att_ctx = _ctx_attn(proj, bp, lp, NA_HEADS, NA_HEADS)
            bias = _na_bias_tables(na_rpb[i], ls // GRID_W)
            att_lat = _na_attn(proj, cache_na_k[:, i].reshape(bs, -1, na_w), cache_na_v[:, i].reshape(bs, -1, na_w),
                               bias, n_ctx, bs, ls)
            s5p = dict(a_re=s5_a_re[i], a_im=s5_a_im[i], log_dt=s5_log_dt[i], b_re=s5_b_re[i], b_im=s5_b_im[i],
                       c_re=s5_c_re[i], c_im=s5_c_im[i], d=s5_d[i], glu_w=s5_glu_w[i], glu_b=s5_glu_b[i])
            s5o, fin = _s5_mixer(proj[:, 3 * na_w:], s5p, state_s5[:, i], bp, lp, bs, ls)
            att = jnp.concatenate([att_ctx, att_lat], axis=0)
            x = _outproj([att, s5o], ab_w_out[i].astype(BF16), x, mod, 2, n_ctx, ls)
            na_k.append(proj[:n_ctx, na_w:2 * na_w].reshape(bp, lp, NA_HEADS, HEAD_DIM))
            na_v.append(proj[:n_ctx, 2 * na_w:3 * na_w].reshape(bp, lp, NA_HEADS, HEAD_DIM))
            s5_fin.append(fin)
        else:
            proj = _normproj(x, mod, 0, norm_mix[layer], swa_w_in[i].astype(BF16), n_ctx, ls)
            sink = swa_sink[i].astype(F32)
            att_ctx = _ctx_attn(proj, bp, lp, SWA_HEADS, SWA_KV_HEADS, sink=sink)
            att_lat = _swa_attn(proj, cache_swa_k[:, i].reshape(bs, -1, swa_kv),
                                cache_swa_v[:, i].reshape(bs, -1, swa_kv), sink, cos, sin_signed, n_ctx, bs, ls)
            att = jnp.concatenate([att_ctx, att_lat], axis=0)
            x = _outproj([att], swa_w_out[i].astype(BF16), x, mod, 2, n_ctx, ls)
            swa_k.append(proj[:n_ctx, swa_q:swa_q + swa_kv].reshape(bp, lp, SWA_KV_HEADS, HEAD_DIM))
            swa_v.append(proj[:n_ctx, swa_q + swa_kv:].reshape(bp, lp, SWA_KV_HEADS, HEAD_DIM))
        act = _ffn_up(x, mod, norm_ffn[layer], ffn_w_up[layer].astype(BF16), ffn_conv_w[layer],
                      ffn_conv_b[layer], n_ctx, lp, ls)
        x = _outproj([act], ffn_w_down[layer].astype(BF16), x, mod, 5, n_ctx, ls)

    y = _rmsnorm(x, final_norm)
    y_prompt = y[:n_ctx].reshape(bp, lp, d)
    y_sample = y[n_ctx:].reshape(bs, ls, d)
    return (y_prompt, y_sample, jnp.stack(na_k, axis=1), jnp.stack(na_v, axis=1), jnp.stack(s5_fin, axis=1),
            jnp.stack(swa_k, axis=1), jnp.stack(swa_v, axis=1))
```

```python
import functools
import math

import jax
import jax.numpy as jnp
import numpy as np
from jax import lax
from jax.experimental import pallas as pl
from jax.experimental.pallas import tpu as pltpu

F32 = jnp.float32
BF16 = jnp.bfloat16

GRID_W = 64
HEAD_DIM = 128
NA_HEADS = 8
NA_WIN_R = 8
NA_WIN_C = 16
S5_GROUP = 16
S5_STATE = 64
SWA_HEADS = 16
SWA_KV_HEADS = 4
SWA_WINDOW = 128
ROPE_BASE = 10000.0
EPS = 1e-6
NEG_INF = -1e30
ATT_SCALE = HEAD_DIM ** -0.5

VMEM_PHYSICAL_MIB = 64
SUBLANES = 8
BF16_SUBLANES = 16
LANES = 128

S5_STEPS = 256
S5_ROWS = 16
S5_GB = 8
S5_GB_IN = S5_GB * S5_GROUP
S5_GB_ST = S5_GB * S5_STATE


def _params(sem, vmem_mib):
    assert vmem_mib <= VMEM_PHYSICAL_MIB
    if sem is None:
        return pltpu.CompilerParams(vmem_limit_bytes=vmem_mib << 20)
    return pltpu.CompilerParams(dimension_semantics=sem, vmem_limit_bytes=vmem_mib << 20)


def _dot(a, b):
    return jnp.dot(a, b, preferred_element_type=F32)


def _dot_t(a, b):
    return lax.dot_general(a, b, (((1,), (1,)), ((), ())), preferred_element_type=F32)


def _sigmoid(x):
    return 1.0 / (1.0 + jnp.exp(-x))


class _Rows:
    def __init__(self, *arrays):
        assert len(arrays) in (1, 2)
        self.arrays = arrays
        self.width = arrays[0].shape[1]
        self.dtype = arrays[0].dtype
        self.rows = sum(a.shape[0] for a in arrays)

    def specs(self, tm, row_of_grid=lambda *g: g[0]):
        if len(self.arrays) == 1:
            return [pl.BlockSpec((tm, self.width), lambda *g: (row_of_grid(*g), 0))]
        na = self.arrays[0].shape[0] // tm
        return [
            pl.BlockSpec((tm, self.width), lambda *g: (jnp.minimum(row_of_grid(*g), na - 1), 0)),
            pl.BlockSpec((tm, self.width), lambda *g: (jnp.maximum(row_of_grid(*g) - na, 0), 0)),
        ]

    def n_refs(self):
        return len(self.arrays)

    def first_tiles(self, tm):
        return self.arrays[0].shape[0] // tm


def _select_rows(refs, tile, first_tiles, consume):
    if len(refs) == 1:
        consume(refs[0])
        return

    @pl.when(tile < first_tiles)
    def _():
        consume(refs[0])

    @pl.when(tile >= first_tiles)
    def _():
        consume(refs[1])


def _staged_rows(refs, tile, first_tiles, stage_ref):
    if len(refs) == 1:
        return refs[0]

    def copy(ref):
        stage_ref[...] = ref[...]
    _select_rows(refs, tile, first_tiles, copy)
    return stage_ref


def _adaln_kernel(c_ref, w_ref, b_ref, o_ref):
    c = c_ref[...]
    s = (c * _sigmoid(c)).astype(BF16)
    o_ref[0] = _dot(s, w_ref[0].astype(BF16)) + b_ref[0]


def _adaln(cond8, ada_w, ada_b):
    depth, d, n = ada_w.shape
    tn = 1024
    return pl.pallas_call(
        _adaln_kernel,
        grid=(depth, n // tn),
        in_specs=[
            pl.BlockSpec((SUBLANES, d), lambda l, j: (0, 0)),
            pl.BlockSpec((1, d, tn), lambda l, j: (l, 0, j)),
            pl.BlockSpec((1, 1, tn), lambda l, j: (l, 0, j)),
        ],
        out_specs=pl.BlockSpec((1, SUBLANES, tn), lambda l, j: (l, 0, j)),
        out_shape=jax.ShapeDtypeStruct((depth, SUBLANES, n), F32),
        compiler_params=_params(("arbitrary", "arbitrary"), 40),
        name="adaln",
    )(cond8, ada_w, ada_b.reshape(depth, 1, n))


MOD_SHIFT1, MOD_SCALE1, MOD_GATE1, MOD_SHIFT2, MOD_SCALE2, MOD_GATE2 = range(6)


def _group_of(row0, n_ctx, lat_len):
    return jnp.maximum((row0 - n_ctx) // lat_len + 1, 0)


def _mod_spec(which, tm, n_ctx, lat_len, d):
    return pl.BlockSpec((1, 1, 1, d), lambda *g: (_group_of(g[0] * tm, n_ctx, lat_len), which, 0, 0))


def _modulated(x, g, shift, scale):
    ms = jnp.mean(x * x, axis=-1, keepdims=True)
    y = x * lax.rsqrt(ms + EPS) * g
    return y * (1.0 + scale) + shift


def _normproj_kernel(*refs, n_x, first_tiles):
    x_refs = refs[:n_x]
    sh_ref, sc_ref, g_ref, w_ref, o_ref, h_ref = refs[n_x:]
    tile = pl.program_id(0)

    @pl.when(pl.program_id(1) == 0)
    def _():
        def fill(x_ref):
            h = _modulated(x_ref[...], g_ref[...], sh_ref[0, 0], sc_ref[0, 0])
            h_ref[...] = h.astype(BF16)
        _select_rows(x_refs, tile, first_tiles, fill)

    o_ref[...] = _dot(h_ref[...], w_ref[...])


def _normproj(x, mod, g, w, n_ctx, lat_len, tm=1024, tn=512):
    t, d = x.rows, x.width
    n = w.shape[1]
    return pl.pallas_call(
        functools.partial(_normproj_kernel, n_x=x.n_refs(), first_tiles=x.first_tiles(tm)),
        grid=(t // tm, n // tn),
        in_specs=x.specs(tm) + [
            _mod_spec(MOD_SHIFT1, tm, n_ctx, lat_len, d),
            _mod_spec(MOD_SCALE1, tm, n_ctx, lat_len, d),
            pl.BlockSpec((1, d), lambda i, j: (0, 0)),
            pl.BlockSpec((d, tn), lambda i, j: (0, j)),
        ],
        out_specs=pl.BlockSpec((tm, tn), lambda i, j: (i, j)),
        out_shape=jax.ShapeDtypeStruct((t, n), F32),
        scratch_shapes=[pltpu.VMEM((tm, d), BF16)],
        compiler_params=_params(("arbitrary", "arbitrary"), 56),
        name="normproj",
    )(*x.arrays, mod, mod, g.reshape(1, d), w)


def _proj_kernel(h_ref, w_ref, o_ref):
    o_ref[...] = _dot(h_ref[...], w_ref[...])


def _proj(h, w, tm=1024, tn=512):
    t, d = h.shape
    n = w.shape[1]
    return pl.pallas_call(
        _proj_kernel,
        grid=(t // tm, n // tn),
        in_specs=[pl.BlockSpec((tm, d), lambda i, j: (i, 0)), pl.BlockSpec((d, tn), lambda i, j: (0, j))],
        out_specs=pl.BlockSpec((tm, tn), lambda i, j: (i, j)),
        out_shape=jax.ShapeDtypeStruct((t, n), F32),
        compiler_params=_params(("arbitrary", "arbitrary"), 48),
        name="proj",
    )(h, w)


COL_CHUNK = 512


def _residual_norm(acc_of_cols, x_of_cols, gate_ref, xo_ref, d, tm):
    ss = jnp.zeros((tm, 1), F32)
    for c0 in range(0, d, COL_CHUNK):
        cols = slice(c0, c0 + COL_CHUNK)
        xn = x_of_cols(cols) + gate_ref[0, 0, :, cols] * acc_of_cols(cols)
        xo_ref[:, cols] = xn
        ss = ss + jnp.sum(xn * xn, axis=-1, keepdims=True)
    return lax.rsqrt(ss * (1.0 / d) + EPS)


def _write_modulated(xo_ref, inv, g_ref, sh_ref, sc_ref, h_ref, d):
    for c0 in range(0, d, COL_CHUNK):
        cols = slice(c0, c0 + COL_CHUNK)
        y = xo_ref[:, cols] * inv * g_ref[:, cols]
        h_ref[:, cols] = (y * (1.0 + sc_ref[0, 0, :, cols]) + sh_ref[0, 0, :, cols]).astype(BF16)


def _outproj_kernel(*refs, part_refs, part_first, n_x, x_first, d, tm):
    pos = 0
    lhs_refs = []
    for n in part_refs:
        lhs_refs.append(refs[pos:pos + n])
        pos += n
    x_refs = refs[pos:pos + n_x]
    pos += n_x
    w_ref, gate_ref, g_ref, sh_ref, sc_ref, xo_ref, h_ref = refs[pos:pos + 7]
    stage = list(refs[pos + 7:])
    tile = pl.program_id(0)

    lhs = []
    for prefs, first in zip(lhs_refs, part_first):
        lhs.append(_staged_rows(prefs, tile, first, stage.pop(0) if len(prefs) > 1 else None))
    x_ref = _staged_rows(x_refs, tile, x_first, stage.pop(0) if n_x > 1 else None)

    def acc_of_cols(cols):
        row = 0
        acc = None
        for l_ref in lhs:
            k = l_ref.shape[1]
            part = _dot(l_ref[...], w_ref[row:row + k, cols])
            acc = part if acc is None else acc + part
            row += k
        return acc

    inv = _residual_norm(acc_of_cols, lambda cols: x_ref[:, cols], gate_ref, xo_ref, d, tm)
    _write_modulated(xo_ref, inv, g_ref, sh_ref, sc_ref, h_ref, d)


def _outproj(parts, w, x, mod, g_next, n_ctx, lat_len, tm=512):
    t, d = x.rows, x.width
    in_specs, args, scratch = [], [], []
    for p in parts:
        in_specs += p.specs(tm)
        args += list(p.arrays)
    in_specs += x.specs(tm)
    args += list(x.arrays)
    in_specs += [
        pl.BlockSpec(w.shape, lambda i: (0, 0), pipeline_mode=pl.Buffered(1)),
        _mod_spec(MOD_GATE1, tm, n_ctx, lat_len, d),
        pl.BlockSpec((1, d), lambda i: (0, 0)),
        _mod_spec(MOD_SHIFT2, tm, n_ctx, lat_len, d),
        _mod_spec(MOD_SCALE2, tm, n_ctx, lat_len, d),
    ]
    args += [w, mod, g_next.reshape(1, d), mod, mod]
    scratch = [pltpu.VMEM((tm, p.width), p.dtype) for p in parts if p.n_refs() > 1]
    if x.n_refs() > 1:
        scratch.append(pltpu.VMEM((tm, d), F32))
    kern = functools.partial(
        _outproj_kernel, part_refs=[p.n_refs() for p in parts], part_first=[p.first_tiles(tm) for p in parts],
        n_x=x.n_refs(), x_first=x.first_tiles(tm), d=d, tm=tm)
    return pl.pallas_call(
        kern,
        grid=(t // tm,),
        in_specs=in_specs,
        out_specs=[pl.BlockSpec((tm, d), lambda i: (i, 0)), pl.BlockSpec((tm, d), lambda i: (i, 0))],
        out_shape=[jax.ShapeDtypeStruct((t, d), F32), jax.ShapeDtypeStruct((t, d), BF16)],
        scratch_shapes=scratch,
        compiler_params=_params(("arbitrary",), 56),
        name="outproj",
    )(*args)


def _ffn_down_kernel(act_ref, w_ref, x_ref, gate_ref, g_ref, *rest, final, d, tm, n_k, first_tiles):
    if final:
        yp_ref, ys_ref, acc_ref, xn_ref = rest
    else:
        sh_ref, sc_ref, xo_ref, h_ref, acc_ref = rest
    tile = pl.program_id(0)
    k = pl.program_id(1)

    @pl.when(k == 0)
    def _():
        acc_ref[...] = _dot(act_ref[...], w_ref[...])

    @pl.when(k > 0)
    def _():
        acc_ref[...] += _dot(act_ref[...], w_ref[...])

    @pl.when(k == n_k - 1)
    def _():
        x_new_ref = xn_ref if final else xo_ref
        inv = _residual_norm(lambda cols: acc_ref[:, cols], lambda cols: x_ref[:, cols], gate_ref, x_new_ref, d, tm)
        if final:
            def emit(o_ref):
                for c0 in range(0, d, COL_CHUNK):
                    cols = slice(c0, c0 + COL_CHUNK)
                    o_ref[:, cols] = xn_ref[:, cols] * inv * g_ref[:, cols]
            _select_rows((yp_ref, ys_ref), tile, first_tiles, emit)
        else:
            _write_modulated(xo_ref, inv, g_ref, sh_ref, sc_ref, h_ref, d)


def _ffn_down(act, w_all, layer, x, mod, mod_next, g_next, n_ctx, lat_len, final, tm=512, tk=512):
    t, d = x.shape
    f = act.shape[1]
    n_k = f // tk
    nc = n_ctx // tm
    in_specs = [
        pl.BlockSpec((tm, tk), lambda i, k: (i, k)),
        pl.BlockSpec((None, tk, d), lambda i, k: (layer, k, 0)),
        pl.BlockSpec((tm, d), lambda i, k: (i, 0)),
        _mod_spec(MOD_GATE2, tm, n_ctx, lat_len, d),
        pl.BlockSpec((1, d), lambda i, k: (0, 0)),
    ]
    args = [act, w_all, x, mod, g_next.reshape(1, d)]
    scratch = [pltpu.VMEM((tm, d), F32)]
    if final:
        out_specs = [
            pl.BlockSpec((tm, d), lambda i, k: (jnp.minimum(i, nc - 1), 0)),
            pl.BlockSpec((tm, d), lambda i, k: (jnp.maximum(i - nc, 0), 0)),
        ]
        out_shape = [jax.ShapeDtypeStruct((n_ctx, d), F32), jax.ShapeDtypeStruct((t - n_ctx, d), F32)]
        scratch.append(pltpu.VMEM((tm, d), F32))
    else:
        in_specs += [_mod_spec(MOD_SHIFT1, tm, n_ctx, lat_len, d), _mod_spec(MOD_SCALE1, tm, n_ctx, lat_len, d)]
        args += [mod_next, mod_next]
        out_specs = [pl.BlockSpec((tm, d), lambda i, k: (i, 0)), pl.BlockSpec((tm, d), lambda i, k: (i, 0))]
        out_shape = [jax.ShapeDtypeStruct((t, d), F32), jax.ShapeDtypeStruct((t, d), BF16)]
    kern = functools.partial(_ffn_down_kernel, final=final, d=d, tm=tm, n_k=n_k, first_tiles=nc)
    return pl.pallas_call(
        kern,
        grid=(t // tm, n_k),
        in_specs=in_specs,
        out_specs=out_specs,
        out_shape=out_shape,
        scratch_shapes=scratch,
        compiler_params=_params(("arbitrary", "arbitrary"), 56),
        name="ffn_down",
    )(*args)


HALO = BF16_SUBLANES


def _ffn_up_kernel(h_ref, hp_ref, hn_ref, wg_ref, wu_ref, cg_ref, cu_ref, bg_ref, bu_ref, o_ref, hb_ref,
                   *, tm, n_ctx, ctx_len, lat_len):
    i = pl.program_id(0)
    row0 = i * tm
    is_ctx = row0 < n_ctx
    assert ctx_len & (ctx_len - 1) == 0 and lat_len & (lat_len - 1) == 0 and n_ctx % lat_len == 0

    def seq_pos(r):
        return jnp.where(is_ctx, r & (ctx_len - 1), r & (lat_len - 1))

    def seq_len():
        return jnp.where(is_ctx, ctx_len, lat_len)

    @pl.when(pl.program_id(1) == 0)
    def _():
        hb_ref[HALO:HALO + tm, :] = h_ref[...]
        keep_prev = (seq_pos(row0) != 0).astype(F32)
        keep_next = (seq_pos(row0 + tm - 1) != seq_len() - 1).astype(F32)
        hb_ref[0:HALO, :] = (hp_ref[...].astype(F32) * keep_prev).astype(BF16)
        hb_ref[HALO + tm:, :] = (hn_ref[...].astype(F32) * keep_next).astype(BF16)

    rows = row0 + lax.broadcasted_iota(jnp.int32, (tm, 1), 0)
    pos = seq_pos(rows)
    keep_p = (pos != 0).astype(F32)
    keep_n = (pos != seq_len() - 1).astype(F32)
    tot = tm + 2 * HALO

    def conv(w_ref, c_ref, b_ref):
        z = _dot(hb_ref[...], w_ref[...])
        zp = pltpu.roll(z, 1, axis=0)[HALO:HALO + tm]
        zn = pltpu.roll(z, tot - 1, axis=0)[HALO:HALO + tm]
        c = c_ref[...]
        return (zp * keep_p) * c[0:1] + z[HALO:HALO + tm] * c[1:2] + (zn * keep_n) * c[2:3] + b_ref[...]

    gate = conv(wg_ref, cg_ref, bg_ref)
    up = conv(wu_ref, cu_ref, bu_ref)
    o_ref[...] = (gate * _sigmoid(gate) * up).astype(BF16)


def _ffn_up(h, w_all, conv_w_all, conv_b_all, layer, n_ctx, ctx_len, lat_len, tm=1024, tn=512):
    t, d = h.shape
    f = w_all.shape[2] // 2
    nj = f // tn
    hb = tm // HALO
    last_blk = t // HALO - 1
    kern = functools.partial(_ffn_up_kernel, tm=tm, n_ctx=n_ctx, ctx_len=ctx_len, lat_len=lat_len)
    return pl.pallas_call(
        kern,
        grid=(t // tm, nj),
        in_specs=[
            pl.BlockSpec((tm, d), lambda i, j: (i, 0)),
            pl.BlockSpec((HALO, d), lambda i, j: (jnp.maximum(i * hb - 1, 0), 0)),
            pl.BlockSpec((HALO, d), lambda i, j: (jnp.minimum((i + 1) * hb, last_blk), 0)),
            pl.BlockSpec((None, d, tn), lambda i, j: (layer, 0, j)),
            pl.BlockSpec((None, d, tn), lambda i, j: (layer, 0, j + nj)),
            pl.BlockSpec((None, 3, tn), lambda i, j: (layer, 0, j)),
            pl.BlockSpec((None, 3, tn), lambda i, j: (layer, 0, j + nj)),
            pl.BlockSpec((None, 1, tn), lambda i, j: (layer, 0, j)),
            pl.BlockSpec((None, 1, tn), lambda i, j: (layer, 0, j + nj)),
        ],
        out_specs=pl.BlockSpec((tm, tn), lambda i, j: (i, j)),
        out_shape=jax.ShapeDtypeStruct((t, f), BF16),
        scratch_shapes=[pltpu.VMEM((tm + 2 * HALO, d), BF16)],
        compiler_params=_params(("arbitrary", "arbitrary"), 48),
        name="ffn_up",
    )(h, h, h, w_all, w_all, conv_w_all, conv_w_all, conv_b_all, conv_b_all)


def _attend(score_blocks, value_blocks, sink=None):
    m = jnp.max(score_blocks[0], axis=-1, keepdims=True)
    for s in score_blocks[1:]:
        m = jnp.maximum(m, jnp.max(s, axis=-1, keepdims=True))
    if sink is not None:
        m = jnp.maximum(m, sink)
    den = jnp.exp(sink - m) if sink is not None else 0.0
    out = None
    for s, v in zip(score_blocks, value_blocks):
        p = jnp.exp(s - m)
        den = den + jnp.sum(p, axis=-1, keepdims=True)
        o = _dot(p.astype(BF16), v)
        out = o if out is None else out + o
    return out / den


def _scaled_q(q):
    return (q * ATT_SCALE).astype(BF16)


def _ctx_attn_kernel(*refs, n_heads, n_kv, has_sink):
    if has_sink:
        sink_ref, q_ref, k_ref, v_ref, o_ref = refs
    else:
        q_ref, k_ref, v_ref, o_ref = refs
    grp = n_heads // n_kv
    for kv in range(n_kv):
        ksl = slice(kv * HEAD_DIM, (kv + 1) * HEAD_DIM)
        k = k_ref[:, ksl].astype(BF16)
        v = v_ref[:, ksl].astype(BF16)
        for gi in range(grp):
            h = kv * grp + gi
            hsl = slice(h * HEAD_DIM, (h + 1) * HEAD_DIM)
            s = _dot_t(_scaled_q(q_ref[:, hsl]), k)
            sink = sink_ref[h] if has_sink else None
            o_ref[:, hsl] = _attend([s], [v], sink).astype(BF16)


def _ctx_attn(proj, n_seq, seq_len, n_heads, n_kv, sink=None):
    qw = n_heads * HEAD_DIM
    kw = n_kv * HEAD_DIM
    assert qw % kw == 0
    kb = qw // kw
    has_sink = sink is not None
    in_specs = [
        pl.BlockSpec((seq_len, qw), lambda b: (b, 0)),
        pl.BlockSpec((seq_len, kw), lambda b: (b, kb)),
        pl.BlockSpec((seq_len, kw), lambda b: (b, kb + 1)),
    ]
    args = [proj, proj, proj]
    if has_sink:
        in_specs = [pl.BlockSpec(memory_space=pltpu.SMEM)] + in_specs
        args = [sink] + args
    return pl.pallas_call(
        functools.partial(_ctx_attn_kernel, n_heads=n_heads, n_kv=n_kv, has_sink=has_sink),
        grid=(n_seq,),
        in_specs=in_specs,
        out_specs=pl.BlockSpec((seq_len, qw), lambda b: (b, 0)),
        out_shape=jax.ShapeDtypeStruct((n_seq * seq_len, qw), BF16),
        compiler_params=_params(("arbitrary",), 32),
        name="ctx_attn",
    )(*args)


NA_QROWS = 8
NA_KROWS = 16
NA_QBLK = NA_QROWS * GRID_W
NA_KBLK = NA_KROWS * GRID_W
NA_HEADS_PER_STEP = 2


def _na_key_row0(blk, rows):
    return np.clip(blk * NA_QROWS - NA_WIN_R // 2, 0, rows - NA_KROWS)


def _na_bias_tables(rpb, rows):
    h = rpb.shape[0]
    nblk = rows // NA_QROWS
    col = np.arange(GRID_W)
    col_start = np.clip(col - NA_WIN_C // 2, 0, GRID_W - NA_WIN_C)
    col_ok = (col[None, :] >= col_start[:, None]) & (col[None, :] < col_start[:, None] + NA_WIN_C)
    dc = np.clip(col[None, :] - col[:, None] + NA_WIN_C - 1, 0, 2 * NA_WIN_C - 2)
    tiles = jnp.take(rpb.astype(F32), jnp.asarray(dc.reshape(-1)), axis=2).reshape(
        h, 2 * NA_WIN_R - 1, GRID_W, GRID_W)
    tiles = jnp.where(jnp.asarray(col_ok)[None, None], tiles, NEG_INF)
    tiles = jnp.concatenate([tiles, jnp.full((h, 1, GRID_W, GRID_W), NEG_INF, F32)], axis=1)
    masked = 2 * NA_WIN_R - 1
    a_idx = np.full((3, NA_QROWS, NA_KROWS), masked, np.int32)
    for v, blk in enumerate((0, 1, nblk - 1)):
        kr0 = _na_key_row0(blk, rows)
        for ql in range(NA_QROWS):
            qr = blk * NA_QROWS + ql
            start = np.clip(qr - NA_WIN_R // 2, 0, rows - NA_WIN_R)
            for kl in range(NA_KROWS):
                kr = kr0 + kl
                if start <= kr < start + NA_WIN_R:
                    a_idx[v, ql, kl] = kr - qr + NA_WIN_R - 1
    t = jnp.take(tiles, jnp.asarray(a_idx.reshape(-1)), axis=1)
    t = t.reshape(h, 3, NA_QROWS, NA_KROWS, GRID_W, GRID_W).transpose(1, 0, 2, 4, 3, 5)
    return t.reshape(3, h, NA_QBLK, NA_KBLK)


def _na_kernel(q_ref, k_ref, v_ref, kc_ref, vc_ref, bias_ref, o_ref, *, rows):
    i = pl.program_id(2)
    kr0 = jnp.clip(i * NA_QROWS - NA_WIN_R // 2, 0, rows - NA_KROWS)
    k0 = pl.multiple_of(kr0 * GRID_W, GRID_W * (NA_WIN_R // 2))
    for hh in range(NA_HEADS_PER_STEP):
        hsl = slice(hh * HEAD_DIM, (hh + 1) * HEAD_DIM)
        q = _scaled_q(q_ref[:, hsl])
        kw = k_ref[pl.ds(k0, NA_KBLK), hsl].astype(BF16)
        vw = v_ref[pl.ds(k0, NA_KBLK), hsl].astype(BF16)
        s_loc = _dot_t(q, kw) + bias_ref[0, hh]
        s_ctx = _dot_t(q, kc_ref[0, :, hsl].astype(BF16))
        o_ref[:, hsl] = _attend([s_loc, s_ctx], [vw, vc_ref[0, :, hsl].astype(BF16)]).astype(BF16)


def _na_attn(proj, cache_k, cache_v, bias, n_ctx, n_lat, lat_len):
    rows = lat_len // GRID_W
    nblk = lat_len // NA_QBLK
    assert n_ctx % lat_len == 0 and n_ctx % NA_QBLK == 0
    qb0 = n_ctx // NA_QBLK
    sb0 = n_ctx // lat_len
    hps = NA_HEADS_PER_STEP
    hw = hps * HEAD_DIM
    hg = NA_HEADS // hps
    c = cache_k.shape[1]
    variant = lambda i: jnp.where(i == 0, 0, jnp.where(i == nblk - 1, 2, 1))
    return pl.pallas_call(
        functools.partial(_na_kernel, rows=rows),
        grid=(n_lat, hg, nblk),
        in_specs=[
            pl.BlockSpec((NA_QBLK, hw), lambda b, hh, i: (qb0 + b * nblk + i, hh)),
            pl.BlockSpec((lat_len, hw), lambda b, hh, i: (sb0 + b, hg + hh)),
            pl.BlockSpec((lat_len, hw), lambda b, hh, i: (sb0 + b, 2 * hg + hh)),
            pl.BlockSpec((1, c, hw), lambda b, hh, i: (b, 0, hh)),
            pl.BlockSpec((1, c, hw), lambda b, hh, i: (b, 0, hh)),
            pl.BlockSpec((1, hps, NA_QBLK, NA_KBLK), lambda b, hh, i: (variant(i), hh, 0, 0)),
        ],
        out_specs=pl.BlockSpec((NA_QBLK, hw), lambda b, hh, i: (b * nblk + i, hh)),
        out_shape=jax.ShapeDtypeStruct((n_lat * lat_len, NA_HEADS * HEAD_DIM), BF16),
        compiler_params=_params(("arbitrary", "arbitrary", "arbitrary"), 56),
        name="na_attn",
    )(proj, proj, proj, cache_k, cache_v, bias)


SWA_QBLK = 512
SWA_KBLK = SWA_QBLK + 2 * SWA_WINDOW


def _rope_tables(length):
    half = HEAD_DIM // 2
    freqs = ROPE_BASE ** (-jnp.arange(0, half, 2, dtype=F32) / half)
    t = jnp.arange(length)
    ang_r = (t // GRID_W).astype(F32)[:, None] * freqs[None, :]
    ang_c = (t % GRID_W).astype(F32)[:, None] * freqs[None, :]
    ang = jnp.concatenate([ang_r, ang_r, ang_c, ang_c], axis=1)
    sign = np.concatenate([-np.ones(half // 2), np.ones(half // 2)] * 2).astype(np.float32)
    return jnp.cos(ang), jnp.sin(ang) * jnp.asarray(sign)[None, :]


def _rope(x, cos, sin_signed):
    lane = lax.broadcasted_iota(jnp.int32, x.shape, 1)
    quarter = HEAD_DIM // 4
    lower = (lane & (2 * quarter - 1)) < quarter
    swapped = jnp.where(lower, pltpu.roll(x, HEAD_DIM - quarter, axis=1), pltpu.roll(x, quarter, axis=1))
    return x * cos + swapped * sin_signed


def _swa_kernel(sink_ref, q_ref, k_ref, v_ref, kc_ref, vc_ref, cos_ref, sin_ref, o_ref, *, lat_len):
    kv = pl.program_id(1)
    i = pl.program_id(2)
    grp = SWA_HEADS // SWA_KV_HEADS
    q0 = pl.multiple_of(i * SWA_QBLK, SWA_QBLK)
    k0 = pl.multiple_of(jnp.clip(i * SWA_QBLK - SWA_WINDOW, 0, lat_len - SWA_KBLK), SWA_WINDOW)
    kw = _rope(k_ref[pl.ds(k0, SWA_KBLK), :], cos_ref[pl.ds(k0, SWA_KBLK), :],
               sin_ref[pl.ds(k0, SWA_KBLK), :]).astype(BF16)
    vw = v_ref[pl.ds(k0, SWA_KBLK), :].astype(BF16)
    kc = kc_ref[0].astype(BF16)
    vc = vc_ref[0].astype(BF16)
    qpos = q0 + lax.broadcasted_iota(jnp.int32, (SWA_QBLK, SWA_KBLK), 0)
    kpos = k0 + lax.broadcasted_iota(jnp.int32, (SWA_QBLK, SWA_KBLK), 1)
    band = jnp.where(jnp.abs(qpos - kpos) <= SWA_WINDOW, 0.0, NEG_INF)
    cos_q = cos_ref[pl.ds(q0, SWA_QBLK), :]
    sin_q = sin_ref[pl.ds(q0, SWA_QBLK), :]
    for gi in range(grp):
        hsl = slice(gi * HEAD_DIM, (gi + 1) * HEAD_DIM)
        q = _scaled_q(_rope(q_ref[:, hsl], cos_q, sin_q))
        s_loc = _dot_t(q, kw) + band
        s_ctx = _dot_t(q, kc)
        sink = sink_ref[kv * grp + gi]
        o_ref[:, hsl] = _attend([s_loc, s_ctx], [vw, vc], sink).astype(BF16)


def _swa_attn(proj, cache_k, cache_v, sink, cos, sin_signed, n_ctx, n_lat, lat_len):
    grp = SWA_HEADS // SWA_KV_HEADS
    gw = grp * HEAD_DIM
    nblk = lat_len // SWA_QBLK
    qb0 = n_ctx // SWA_QBLK
    sb0 = n_ctx // lat_len
    kcol0 = SWA_HEADS
    vcol0 = SWA_HEADS + SWA_KV_HEADS
    c = cache_k.shape[1]
    return pl.pallas_call(
        functools.partial(_swa_kernel, lat_len=lat_len),
        grid=(n_lat, SWA_KV_HEADS, nblk),
        in_specs=[
            pl.BlockSpec(memory_space=pltpu.SMEM),
            pl.BlockSpec((SWA_QBLK, gw), lambda b, kv, i: (qb0 + b * nblk + i, kv)),
            pl.BlockSpec((lat_len, HEAD_DIM), lambda b, kv, i: (sb0 + b, kcol0 + kv)),
            pl.BlockSpec((lat_len, HEAD_DIM), lambda b, kv, i: (sb0 + b, vcol0 + kv)),
            pl.BlockSpec((1, c, HEAD_DIM), lambda b, kv, i: (b, 0, kv)),
            pl.BlockSpec((1, c, HEAD_DIM), lambda b, kv, i: (b, 0, kv)),
            pl.BlockSpec((lat_len, HEAD_DIM), lambda b, kv, i: (0, 0)),
            pl.BlockSpec((lat_len, HEAD_DIM), lambda b, kv, i: (0, 0)),
        ],
        out_specs=pl.BlockSpec((SWA_QBLK, gw), lambda b, kv, i: (b * nblk + i, kv)),
        out_shape=jax.ShapeDtypeStruct((n_lat * lat_len, SWA_HEADS * HEAD_DIM), BF16),
        compiler_params=_params(("arbitrary", "arbitrary", "arbitrary"), 48),
        name="swa_attn",
    )(sink, proj, proj, proj, cache_k, cache_v, cos, sin_signed)


def _s5_disc_kernel(are_ref, aim_ref, ldt_ref, bre_ref, bim_ref, abre_ref, abim_ref, bbre_ref, bbim_ref):
    a_re = are_ref[...]
    a_im = aim_ref[...]
    dt = jnp.exp(ldt_ref[...])
    mag = jnp.exp(a_re * dt)
    ang = a_im * dt
    ab_re = mag * jnp.cos(ang)
    ab_im = mag * jnp.sin(ang)
    den = a_re * a_re + a_im * a_im
    n_re = ab_re - 1.0
    f_re = (n_re * a_re + ab_im * a_im) / den
    f_im = (ab_im * a_re - n_re * a_im) / den
    b_re = bre_ref[...]
    b_im = bim_ref[...]
    abre_ref[...] = ab_re
    abim_ref[...] = ab_im
    bbre_ref[...] = f_re * b_re - f_im * b_im
    bbim_ref[...] = f_re * b_im + f_im * b_re


def _s5_discretise(a_re, a_im, log_dt, b_re, b_im):
    nd, g, p, c = b_re.shape
    shape2 = (nd * g, p * c)
    expand = lambda a: jnp.broadcast_to(a[..., None], (nd, g, p, c)).reshape(shape2)
    ldt = jnp.broadcast_to(log_dt[:, :, None, None], (nd, g, p, c)).reshape(shape2)
    out = pl.pallas_call(
        _s5_disc_kernel,
        out_shape=[jax.ShapeDtypeStruct(shape2, F32)] * 4,
        compiler_params=_params(None, 16),
        name="s5_discretise",
    )(expand(a_re), expand(a_im), ldt, b_re.reshape(shape2), b_im.reshape(shape2))
    ab_re, ab_im, bb_re, bb_im = [o.reshape(nd, g, p, c) for o in out]
    return ab_re[..., 0], ab_im[..., 0], bb_re, bb_im


def _block_diag(m, gb):
    g, a, b = m.shape
    m = m.reshape(g // gb, gb, a, b)
    eye = jnp.eye(gb, dtype=m.dtype)
    out = m[:, :, :, None, :] * eye[None, :, None, :, None]
    return out.reshape(g // gb, gb * a, gb * b)


def _cmul_add(ar, ai, xr, xi, br, bi):
    return ar * xr - ai * xi + br, ar * xi + ai * xr + bi


def _s5_kernel(*refs, latent):
    if latent:
        (u_ref, bbr_ref, bbi_ref, ccr_ref, cci_ref, ar_ref, ai_ref, d_ref, h0_ref,
         y_ref, xr_s, xi_s, cr_s, ci_s) = refs
    else:
        (u_ref, bbr_ref, bbi_ref, ccr_ref, cci_ref, ar_ref, ai_ref, d_ref,
         y_ref, fin_ref, xr_s, xi_s) = refs
    rows, steps = S5_ROWS, S5_STEPS
    u = u_ref[0]
    ub = u.astype(BF16)
    y = d_ref[0] * u
    for dirn in range(2):
        reverse = dirn == 1
        xr_s[...] = _dot(ub, bbr_ref[dirn, 0])
        xi_s[...] = _dot(ub, bbi_ref[dirn, 0])
        a_r1 = ar_ref[dirn, 0]
        a_i1 = ai_ref[dirn, 0]
        a_r = jnp.broadcast_to(a_r1, (rows, S5_GB_ST))
        a_i = jnp.broadcast_to(a_i1, (rows, S5_GB_ST))

        def time_rows(s):
            t = (steps - 1 - s) if reverse else s
            if isinstance(t, int):
                return pl.ds(t * rows, rows)
            return pl.ds(pl.multiple_of(t * rows, rows), rows)

        first = time_rows(0)
        final = time_rows(steps - 1)

        def scan_body(s, carry):
            pr, pi = carry
            sl = time_rows(s)
            nr, ni = _cmul_add(a_r, a_i, pr, pi, xr_s[sl, :], xi_s[sl, :])
            xr_s[sl, :] = nr
            xi_s[sl, :] = ni
            return nr, ni

        lax.fori_loop(1, steps, scan_body, (xr_s[first, :], xi_s[first, :]), unroll=4)

        if not latent:
            fin_ref[0, dirn, 0] = xr_s[final, :]
            fin_ref[0, dirn, 1] = xi_s[final, :]
        else:
            p_r, p_i = a_r1, a_i1
            for _ in range(int(math.log2(steps))):
                p_r, p_i = p_r * p_r - p_i * p_i, 2.0 * p_r * p_i
            assert 1 << int(math.log2(steps)) == steps
            e_r = xr_s[final, :]
            e_i = xi_s[final, :]
            c_r = h0_ref[0, 2 * dirn]
            c_i = h0_ref[0, 2 * dirn + 1]
            order = range(rows - 1, -1, -1) if reverse else range(rows)
            for j in order:
                cr_s[j:j + 1, :] = c_r
                ci_s[j:j + 1, :] = c_i
                c_r, c_i = _cmul_add(p_r, p_i, c_r, c_i, e_r[j:j + 1], e_i[j:j + 1])
            zero = jnp.zeros((rows, S5_GB_ST), F32)
            d_r, d_i = _cmul_add(a_r, a_i, cr_s[...], ci_s[...], zero, zero)

            def fix_body(s, carry):
                dr, di = carry
                sl = time_rows(s)
                xr_s[sl, :] = xr_s[sl, :] + dr
                xi_s[sl, :] = xi_s[sl, :] + di
                return _cmul_add(a_r, a_i, dr, di, zero, zero)

            lax.fori_loop(0, steps, fix_body, (d_r, d_i), unroll=4)

        y = y + _dot(xr_s[...].astype(BF16), ccr_ref[dirn, 0]) - _dot(xi_s[...].astype(BF16), cci_ref[dirn, 0])
    y_ref[0] = y


def _s5_scan(u_tm, tile0, tiles, bbr, bbi, ccr, cci, ab_re, ab_im, d, h0=None):
    _, tr, w = u_tm.shape
    assert tr == S5_STEPS * S5_ROWS
    nb = w // S5_GB_IN
    latent = h0 is not None
    in_specs = [
        pl.BlockSpec((1, tr, S5_GB_IN), lambda t, gb: (tile0 + t, 0, gb)),
        pl.BlockSpec((2, 1, S5_GB_IN, S5_GB_ST), lambda t, gb: (0, gb, 0, 0)),
        pl.BlockSpec((2, 1, S5_GB_IN, S5_GB_ST), lambda t, gb: (0, gb, 0, 0)),
        pl.BlockSpec((2, 1, S5_GB_ST, S5_GB_IN), lambda t, gb: (0, gb, 0, 0)),
        pl.BlockSpec((2, 1, S5_GB_ST, S5_GB_IN), lambda t, gb: (0, gb, 0, 0)),
        pl.BlockSpec((2, 1, 1, S5_GB_ST), lambda t, gb: (0, gb, 0, 0)),
        pl.BlockSpec((2, 1, 1, S5_GB_ST), lambda t, gb: (0, gb, 0, 0)),
        pl.BlockSpec((1, 1, S5_GB_IN), lambda t, gb: (gb, 0, 0)),
    ]
    args = [u_tm, bbr, bbi, ccr, cci, ab_re, ab_im, d]
    y_spec = pl.BlockSpec((1, tr, S5_GB_IN), lambda t, gb: (t, 0, gb))
    y_shape = jax.ShapeDtypeStruct((tiles, tr, w), F32)
    scratch = [pltpu.VMEM((tr, S5_GB_ST), F32), pltpu.VMEM((tr, S5_GB_ST), F32)]
    if latent:
        in_specs.append(pl.BlockSpec((1, 4, 1, S5_GB_ST), lambda t, gb: (t, 0, 0, gb)))
        args.append(h0)
        out_specs, out_shape = y_spec, y_shape
        scratch += [pltpu.VMEM((S5_ROWS, S5_GB_ST), F32), pltpu.VMEM((S5_ROWS, S5_GB_ST), F32)]
    else:
        out_specs = [y_spec, pl.BlockSpec((1, 2, 2, S5_ROWS, S5_GB_ST), lambda t, gb: (t, 0, 0, 0, gb))]
        out_shape = [y_shape, jax.ShapeDtypeStruct((tiles, 2, 2, S5_ROWS, nb * S5_GB_ST), F32)]
    return pl.pallas_call(
        functools.partial(_s5_kernel, latent=latent),
        grid=(tiles, nb),
        in_specs=in_specs,
        out_specs=out_specs,
        out_shape=out_shape,
        scratch_shapes=scratch,
        compiler_params=_params(("arbitrary", "arbitrary"), 48),
        name="s5_scan_lat" if latent else "s5_scan_ctx",
    )(*args)


def _glu_kernel(*refs, n_y, first_tiles):
    y_refs = refs[:n_y]
    w_ref, b_ref, o_ref = refs[n_y:]

    def run(y_ref):
        y = y_ref[...]
        g = 0.5 * y * (1.0 + jnp.tanh(math.sqrt(2.0 / math.pi) * (y + 0.044715 * (y * y * y))))
        z = _dot(g.astype(BF16), w_ref[...]) + b_ref[...]
        o_ref[...] = (g * _sigmoid(z)).astype(BF16)

    _select_rows(y_refs, pl.program_id(0), first_tiles, run)


def _glu(y, w, b, tm=1024):
    t, n = y.rows, y.width
    return pl.pallas_call(
        functools.partial(_glu_kernel, n_y=y.n_refs(), first_tiles=y.first_tiles(tm)),
        grid=(t // tm,),
        in_specs=y.specs(tm) + [
            pl.BlockSpec((n, n), lambda i: (0, 0)),
            pl.BlockSpec((1, n), lambda i: (0, 0)),
        ],
        out_specs=pl.BlockSpec((tm, n), lambda i: (i, 0)),
        out_shape=jax.ShapeDtypeStruct((t, n), BF16),
        compiler_params=_params(("arbitrary",), 32),
        name="s5_glu",
    )(*y.arrays, w, b.reshape(1, n))


def _s5_mixer(u, p, state, n_ctx_seq, ctx_len, n_lat, lat_len):
    t, w = u.shape
    g = w // S5_GROUP
    assert ctx_len == S5_STEPS and lat_len == S5_STEPS * S5_ROWS and n_ctx_seq % S5_ROWS == 0
    ab_re, ab_im, bb_re, bb_im = _s5_discretise(p["a_re"], p["a_im"], p["log_dt"], p["b_re"], p["b_im"])
    nb = g // S5_GB
    blocks = lambda m: jnp.stack([_block_diag(m[d].transpose(0, 2, 1), S5_GB) for d in range(2)]).astype(BF16)
    bbr, bbi = blocks(bb_re), blocks(bb_im)
    ccr, cci = blocks(p["c_re"].astype(F32)), blocks(p["c_im"].astype(F32))
    a_r = ab_re.reshape(2, nb, 1, S5_GB_ST)
    a_i = ab_im.reshape(2, nb, 1, S5_GB_ST)
    d = p["d"].astype(F32).reshape(nb, 1, S5_GB_IN)
    tiles = t // (S5_ROWS * S5_STEPS)
    ctx_tiles = n_ctx_seq // S5_ROWS
    tr = S5_STEPS * S5_ROWS
    u_tm = u.reshape(tiles, S5_ROWS, S5_STEPS, w).transpose(0, 2, 1, 3).reshape(tiles, tr, w)
    y_ctx, fin = _s5_scan(u_tm, 0, ctx_tiles, bbr, bbi, ccr, cci, a_r, a_i, d)
    h0 = state.astype(F32).transpose(0, 1, 4, 2, 3).reshape(n_lat, 4, 1, g * S5_STATE)
    y_lat = _s5_scan(u_tm, ctx_tiles, tiles - ctx_tiles, bbr, bbi, ccr, cci, a_r, a_i, d, h0=h0)
    y = _Rows(y_ctx.reshape(ctx_tiles * tr, w), y_lat.reshape((tiles - ctx_tiles) * tr, w))
    o = _glu(y, p["glu_w"].astype(BF16), p["glu_b"].astype(F32))
    o = o.reshape(tiles, S5_STEPS, S5_ROWS, w).transpose(0, 2, 1, 3).reshape(t, w)
    fin = fin.transpose(0, 3, 1, 4, 2).reshape(n_ctx_seq, 2, g, S5_STATE, 2)
    return o, fin


def kernel(x_prompt, x_sample, c, cache_na_k, cache_na_v, state_s5, cache_swa_k, cache_swa_v, c_ctx, norm_mix, norm_ffn, ada_w, ada_b, ab_w_in, ab_w_out, na_rpb, s5_a_re, s5_a_im, s5_log_dt, s5_b_re, s5_b_im, s5_c_re, s5_c_im, s5_d, s5_glu_w, s5_glu_b, swa_w_in, swa_w_out, swa_sink, ffn_w_up, ffn_conv_w, ffn_conv_b, ffn_w_down, final_norm):
    bp, lp, d = x_prompt.shape
    bs, ls, _ = x_sample.shape
    depth = ada_w.shape[0]
    n_ctx = bp * lp
    n_lat = bs * ls
    na_w = NA_HEADS * HEAD_DIM
    swa_q = SWA_HEADS * HEAD_DIM
    swa_kv = SWA_KV_HEADS * HEAD_DIM
    assert depth == 2, "layer pattern below is written for one A/B layer followed by one C layer"

    cond = jnp.concatenate([c_ctx[None, :], c, jnp.zeros((SUBLANES - 1 - bs, d), F32)], axis=0)
    mods = _adaln(cond, ada_w, ada_b).reshape(depth, SUBLANES, 6, 1, d)
    cos, sin_signed = _rope_tables(ls)
    w_up = ffn_w_up.astype(BF16)
    w_down = ffn_w_down.astype(BF16)
    conv_b = ffn_conv_b.reshape(depth, 1, -1)

    x0 = _Rows(x_prompt.reshape(n_ctx, d), x_sample.reshape(n_lat, d))
    proj = _normproj(x0, mods[0], norm_mix[0], ab_w_in[0].astype(BF16), n_ctx, ls)
    att_ctx = _ctx_attn(proj, bp, lp, NA_HEADS, NA_HEADS)
    bias = _na_bias_tables(na_rpb[0], ls // GRID_W)
    att_lat = _na_attn(proj, cache_na_k[:, 0].reshape(bs, -1, na_w), cache_na_v[:, 0].reshape(bs, -1, na_w),
                       bias, n_ctx, bs, ls)
    s5p = dict(a_re=s5_a_re[0], a_im=s5_a_im[0], log_dt=s5_log_dt[0], b_re=s5_b_re[0], b_im=s5_b_im[0],
               c_re=s5_c_re[0], c_im=s5_c_im[0], d=s5_d[0], glu_w=s5_glu_w[0], glu_b=s5_glu_b[0])
    s5o, s5_fin = _s5_mixer(proj[:, 3 * na_w:], s5p, state_s5[:, 0], bp, lp, bs, ls)
    x, h = _outproj([_Rows(att_ctx, att_lat), _Rows(s5o)], ab_w_out[0].astype(BF16), x0, mods[0], norm_ffn[0],
                    n_ctx, ls)
    na_k = proj[:n_ctx, na_w:2 * na_w].reshape(bp, 1, lp, NA_HEADS, HEAD_DIM)
    na_v = proj[:n_ctx, 2 * na_w:3 * na_w].reshape(bp, 1, lp, NA_HEADS, HEAD_DIM)
    act = _ffn_up(h, w_up, ffn_conv_w, conv_b, 0, n_ctx, lp, ls)
    x, h = _ffn_down(act, w_down, 0, x, mods[0], mods[1], norm_mix[1], n_ctx, ls, final=False)

    proj = _proj(h, swa_w_in[0].astype(BF16))
    sink = swa_sink[0].astype(F32)
    att_ctx = _ctx_attn(proj, bp, lp, SWA_HEADS, SWA_KV_HEADS, sink=sink)
    att_lat = _swa_attn(proj, cache_swa_k[:, 0].reshape(bs, -1, swa_kv), cache_swa_v[:, 0].reshape(bs, -1, swa_kv),
                        sink, cos, sin_signed, n_ctx, bs, ls)
    x, h = _outproj([_Rows(att_ctx, att_lat)], swa_w_out[0].astype(BF16), _Rows(x), mods[1], norm_ffn[1], n_ctx, ls)
    swa_k = proj[:n_ctx, swa_q:swa_q + swa_kv].reshape(bp, 1, lp, SWA_KV_HEADS, HEAD_DIM)
    swa_v = proj[:n_ctx, swa_q + swa_kv:].reshape(bp, 1, lp, SWA_KV_HEADS, HEAD_DIM)
    act = _ffn_up(h, w_up, ffn_conv_w, conv_b, 1, n_ctx, lp, ls)
    y_p, y_s = _ffn_down(act, w_down, 1, x, mods[1], None, final_norm, n_ctx, ls, final=True)

    return (y_p.reshape(bp, lp, d), y_s.reshape(bs, ls, d), na_k, na_v, s5_fin[:, None], swa_k, swa_v)
```

```python
import functools
import math

import jax
import jax.numpy as jnp
import numpy as np
from jax import lax
from jax.experimental import pallas as pl
from jax.experimental.pallas import tpu as pltpu

F32 = jnp.float32
BF16 = jnp.bfloat16

GRID_W = 64
HEAD_DIM = 128
NA_HEADS = 8
NA_WIN_R = 8
NA_WIN_C = 16
S5_GROUP = 16
S5_STATE = 64
SWA_HEADS = 16
SWA_KV_HEADS = 4
SWA_WINDOW = 128
ROPE_BASE = 10000.0
EPS = 1e-6
NEG_INF = -1e30
ATT_SCALE = HEAD_DIM ** -0.5

VMEM_PHYSICAL_MIB = 64
SUBLANES = 8
BF16_SUBLANES = 16
LANES = 128

S5_STEPS = 256
S5_ROWS = 16
S5_GB = 8
S5_GB_IN = S5_GB * S5_GROUP
S5_GB_ST = S5_GB * S5_STATE


def _params(sem, vmem_mib):
    assert vmem_mib <= VMEM_PHYSICAL_MIB
    if sem is None:
        return pltpu.CompilerParams(vmem_limit_bytes=vmem_mib << 20)
    return pltpu.CompilerParams(dimension_semantics=sem, vmem_limit_bytes=vmem_mib << 20)


def _dot(a, b):
    return jnp.dot(a, b, preferred_element_type=F32)


def _dot_t(a, b):
    return lax.dot_general(a, b, (((1,), (1,)), ((), ())), preferred_element_type=F32)


def _sigmoid(x):
    return 1.0 / (1.0 + jnp.exp(-x))


class _Rows:
    def __init__(self, *arrays):
        assert len(arrays) in (1, 2)
        self.arrays = arrays
        self.width = arrays[0].shape[1]
        self.dtype = arrays[0].dtype
        self.rows = sum(a.shape[0] for a in arrays)

    def specs(self, tm, row_of_grid=lambda *g: g[0]):
        if len(self.arrays) == 1:
            return [pl.BlockSpec((tm, self.width), lambda *g: (row_of_grid(*g), 0))]
        na = self.arrays[0].shape[0] // tm
        return [
            pl.BlockSpec((tm, self.width), lambda *g: (jnp.minimum(row_of_grid(*g), na - 1), 0)),
            pl.BlockSpec((tm, self.width), lambda *g: (jnp.maximum(row_of_grid(*g) - na, 0), 0)),
        ]

    def n_refs(self):
        return len(self.arrays)

    def first_tiles(self, tm):
        return self.arrays[0].shape[0] // tm


def _select_rows(refs, tile, first_tiles, consume):
    if len(refs) == 1:
        consume(refs[0])
        return

    @pl.when(tile < first_tiles)
    def _():
        consume(refs[0])

    @pl.when(tile >= first_tiles)
    def _():
        consume(refs[1])


def _staged_rows(refs, tile, first_tiles, stage_ref):
    if len(refs) == 1:
        return refs[0]

    def copy(ref):
        stage_ref[...] = ref[...]
    _select_rows(refs, tile, first_tiles, copy)
    return stage_ref


def _adaln_kernel(c_ref, w_ref, b_ref, o_ref):
    c = c_ref[...]
    s = (c * _sigmoid(c)).astype(BF16)
    o_ref[0] = _dot(s, w_ref[0].astype(BF16)) + b_ref[0]


def _adaln(cond8, ada_w, ada_b):
    depth, d, n = ada_w.shape
    tn = 1024
    return pl.pallas_call(
        _adaln_kernel,
        grid=(depth, n // tn),
        in_specs=[
            pl.BlockSpec((SUBLANES, d), lambda l, j: (0, 0)),
            pl.BlockSpec((1, d, tn), lambda l, j: (l, 0, j)),
            pl.BlockSpec((1, 1, tn), lambda l, j: (l, 0, j)),
        ],
        out_specs=pl.BlockSpec((1, SUBLANES, tn), lambda l, j: (l, 0, j)),
        out_shape=jax.ShapeDtypeStruct((depth, SUBLANES, n), F32),
        compiler_params=_params(("arbitrary", "arbitrary"), 40),
        name="adaln",
    )(cond8, ada_w, ada_b.reshape(depth, 1, n))


MOD_SHIFT1, MOD_SCALE1, MOD_GATE1, MOD_SHIFT2, MOD_SCALE2, MOD_GATE2 = range(6)


def _group_of(row0, n_ctx, lat_len):
    return jnp.maximum((row0 - n_ctx) // lat_len + 1, 0)


def _mod_spec(which, tm, n_ctx, lat_len, d):
    return pl.BlockSpec((1, 1, 1, d), lambda *g: (_group_of(g[0] * tm, n_ctx, lat_len), which, 0, 0))


def _modulated(x, g, shift, scale):
    ms = jnp.mean(x * x, axis=-1, keepdims=True)
    y = x * lax.rsqrt(ms + EPS) * g
    return y * (1.0 + scale) + shift


def _norm_mod_kernel(*refs, n_x, first_tiles):
    x_refs = refs[:n_x]
    sh_ref, sc_ref, g_ref, o_ref = refs[n_x:]

    def run(x_ref):
        o_ref[...] = _modulated(x_ref[...], g_ref[...], sh_ref[0, 0], sc_ref[0, 0]).astype(BF16)

    _select_rows(x_refs, pl.program_id(0), first_tiles, run)


def _norm_mod(x, mod, g, n_ctx, lat_len, tm=512):
    t, d = x.rows, x.width
    return pl.pallas_call(
        functools.partial(_norm_mod_kernel, n_x=x.n_refs(), first_tiles=x.first_tiles(tm)),
        grid=(t // tm,),
        in_specs=x.specs(tm) + [
            _mod_spec(MOD_SHIFT1, tm, n_ctx, lat_len, d),
            _mod_spec(MOD_SCALE1, tm, n_ctx, lat_len, d),
            pl.BlockSpec((1, d), lambda i: (0, 0)),
        ],
        out_specs=pl.BlockSpec((tm, d), lambda i: (i, 0)),
        out_shape=jax.ShapeDtypeStruct((t, d), BF16),
        compiler_params=_params(("arbitrary",), 40),
        name="norm_mod",
    )(*x.arrays, mod, mod, g.reshape(1, d))


def _proj_kernel(h_ref, w_ref, o_ref):
    o_ref[...] = _dot(h_ref[...], w_ref[...])


def _proj(h, w, tm=1024, tn=1024):
    t, d = h.shape
    n = w.shape[1]
    return pl.pallas_call(
        _proj_kernel,
        grid=(t // tm, n // tn),
        in_specs=[pl.BlockSpec((tm, d), lambda i, j: (i, 0)), pl.BlockSpec((d, tn), lambda i, j: (0, j))],
        out_specs=pl.BlockSpec((tm, tn), lambda i, j: (i, j)),
        out_shape=jax.ShapeDtypeStruct((t, n), F32),
        compiler_params=_params(("arbitrary", "arbitrary"), 48),
        name="proj",
    )(h, w)


COL_CHUNK = 512


def _residual_norm(acc_of_cols, x_of_cols, gate_ref, xo_ref, d, tm):
    ss = jnp.zeros((tm, 1), F32)
    for c0 in range(0, d, COL_CHUNK):
        cols = slice(c0, c0 + COL_CHUNK)
        xn = x_of_cols(cols) + gate_ref[0, 0, :, cols] * acc_of_cols(cols)
        xo_ref[:, cols] = xn
        ss = ss + jnp.sum(xn * xn, axis=-1, keepdims=True)
    return lax.rsqrt(ss * (1.0 / d) + EPS)


def _write_modulated(xo_ref, inv, g_ref, sh_ref, sc_ref, h_ref, d):
    for c0 in range(0, d, COL_CHUNK):
        cols = slice(c0, c0 + COL_CHUNK)
        y = xo_ref[:, cols] * inv * g_ref[:, cols]
        h_ref[:, cols] = (y * (1.0 + sc_ref[0, 0, :, cols]) + sh_ref[0, 0, :, cols]).astype(BF16)


def _outproj_kernel(*refs, part_refs, part_first, n_x, x_first, d, tm):
    pos = 0
    lhs_refs = []
    for n in part_refs:
        lhs_refs.append(refs[pos:pos + n])
        pos += n
    x_refs = refs[pos:pos + n_x]
    pos += n_x
    w_ref, gate_ref, g_ref, sh_ref, sc_ref, xo_ref, h_ref = refs[pos:pos + 7]
    stage = list(refs[pos + 7:])
    tile = pl.program_id(0)

    lhs = []
    for prefs, first in zip(lhs_refs, part_first):
        lhs.append(_staged_rows(prefs, tile, first, stage.pop(0) if len(prefs) > 1 else None))
    x_ref = _staged_rows(x_refs, tile, x_first, stage.pop(0) if n_x > 1 else None)

    def acc_of_cols(cols):
        row = 0
        acc = None
        for l_ref in lhs:
            k = l_ref.shape[1]
            part = _dot(l_ref[...], w_ref[row:row + k, cols])
            acc = part if acc is None else acc + part
            row += k
        return acc

    inv = _residual_norm(acc_of_cols, lambda cols: x_ref[:, cols], gate_ref, xo_ref, d, tm)
    _write_modulated(xo_ref, inv, g_ref, sh_ref, sc_ref, h_ref, d)


def _outproj(parts, w, x, mod, g_next, n_ctx, lat_len, tm=512):
    t, d = x.rows, x.width
    in_specs, args, scratch = [], [], []
    for p in parts:
        in_specs += p.specs(tm)
        args += list(p.arrays)
    in_specs += x.specs(tm)
    args += list(x.arrays)
    in_specs += [
        pl.BlockSpec(w.shape, lambda i: (0, 0), pipeline_mode=pl.Buffered(1)),
        _mod_spec(MOD_GATE1, tm, n_ctx, lat_len, d),
        pl.BlockSpec((1, d), lambda i: (0, 0)),
        _mod_spec(MOD_SHIFT2, tm, n_ctx, lat_len, d),
        _mod_spec(MOD_SCALE2, tm, n_ctx, lat_len, d),
    ]
    args += [w, mod, g_next.reshape(1, d), mod, mod]
    scratch = [pltpu.VMEM((tm, p.width), p.dtype) for p in parts if p.n_refs() > 1]
    if x.n_refs() > 1:
        scratch.append(pltpu.VMEM((tm, d), F32))
    kern = functools.partial(
        _outproj_kernel, part_refs=[p.n_refs() for p in parts], part_first=[p.first_tiles(tm) for p in parts],
        n_x=x.n_refs(), x_first=x.first_tiles(tm), d=d, tm=tm)
    return pl.pallas_call(
        kern,
        grid=(t // tm,),
        in_specs=in_specs,
        out_specs=[pl.BlockSpec((tm, d), lambda i: (i, 0)), pl.BlockSpec((tm, d), lambda i: (i, 0))],
        out_shape=[jax.ShapeDtypeStruct((t, d), F32), jax.ShapeDtypeStruct((t, d), BF16)],
        scratch_shapes=scratch,
        compiler_params=_params(("arbitrary",), 56),
        name="outproj",
    )(*args)


def _ffn_down_kernel(act_ref, w_ref, x_ref, gate_ref, g_ref, *rest, final, d, tm, n_k):
    if final:
        (acc_ref,) = rest
    else:
        sh_ref, sc_ref, acc_ref, h_ref = rest
    k = pl.program_id(1)

    @pl.when(k == 0)
    def _():
        for c0 in range(0, d, COL_CHUNK):
            acc_ref[:, c0:c0 + COL_CHUNK] = _dot(act_ref[...], w_ref[:, c0:c0 + COL_CHUNK])

    @pl.when(k > 0)
    def _():
        for c0 in range(0, d, COL_CHUNK):
            acc_ref[:, c0:c0 + COL_CHUNK] += _dot(act_ref[...], w_ref[:, c0:c0 + COL_CHUNK])

    @pl.when(k == n_k - 1)
    def _():
        inv = _residual_norm(lambda cols: acc_ref[:, cols], lambda cols: x_ref[:, cols], gate_ref, acc_ref, d, tm)
        if final:
            for c0 in range(0, d, COL_CHUNK):
                cols = slice(c0, c0 + COL_CHUNK)
                acc_ref[:, cols] = acc_ref[:, cols] * inv * g_ref[:, cols]
        else:
            _write_modulated(acc_ref, inv, g_ref, sh_ref, sc_ref, h_ref, d)


def _ffn_down(act, w_all, layer, x, mod, mod_next, g_next, n_ctx, lat_len, row0=0, n_rows=None, tm=1024):
    t, d = x.shape
    f = act.shape[1]
    final = mod_next is None
    n_rows = t if n_rows is None else n_rows
    tk = f // 4
    assert f % tk == 0 and tk % LANES == 0 and row0 % tm == 0 and n_rows % tm == 0
    n_k = f // tk
    t0 = row0 // tm
    mod_spec = lambda which: pl.BlockSpec(
        (1, 1, 1, d), lambda i, k: (_group_of((i + t0) * tm, n_ctx, lat_len), which, 0, 0))
    in_specs = [
        pl.BlockSpec((tm, tk), lambda i, k: (i + t0, k)),
        pl.BlockSpec((None, tk, d), lambda i, k: (layer, k, 0)),
        pl.BlockSpec((tm, d), lambda i, k: (i + t0, 0), pipeline_mode=pl.Buffered(1)),
        mod_spec(MOD_GATE2),
        pl.BlockSpec((1, d), lambda i, k: (0, 0)),
    ]
    args = [act, w_all, x, mod, g_next.reshape(1, d)]
    if final:
        out_specs = pl.BlockSpec((tm, d), lambda i, k: (i, 0))
        out_shape = jax.ShapeDtypeStruct((n_rows, d), F32)
    else:
        in_specs += [mod_spec(MOD_SHIFT1), mod_spec(MOD_SCALE1)]
        args += [mod_next, mod_next]
        out_specs = [pl.BlockSpec((tm, d), lambda i, k: (i, 0)), pl.BlockSpec((tm, d), lambda i, k: (i, 0))]
        out_shape = [jax.ShapeDtypeStruct((n_rows, d), F32), jax.ShapeDtypeStruct((n_rows, d), BF16)]
    kern = functools.partial(_ffn_down_kernel, final=final, d=d, tm=tm, n_k=n_k)
    return pl.pallas_call(
        kern,
        grid=(n_rows // tm, n_k),
        in_specs=in_specs,
        out_specs=out_specs,
        out_shape=out_shape,
        compiler_params=_params(("arbitrary", "arbitrary"), 60),
        name="ffn_down",
    )(*args)


HALO = BF16_SUBLANES


def _ffn_up_kernel(h_ref, hp_ref, hn_ref, wg_ref, wu_ref, cg_ref, cu_ref, bg_ref, bu_ref, o_ref, hb_ref,
                   *, tm, n_ctx, ctx_len, lat_len):
    i = pl.program_id(0)
    row0 = i * tm
    is_ctx = row0 < n_ctx
    assert ctx_len & (ctx_len - 1) == 0 and lat_len & (lat_len - 1) == 0 and n_ctx % lat_len == 0

    def seq_pos(r):
        return jnp.where(is_ctx, r & (ctx_len - 1), r & (lat_len - 1))

    def seq_len():
        return jnp.where(is_ctx, ctx_len, lat_len)

    @pl.when(pl.program_id(1) == 0)
    def _():
        hb_ref[HALO:HALO + tm, :] = h_ref[...]
        keep_prev = (seq_pos(row0) != 0).astype(F32)
        keep_next = (seq_pos(row0 + tm - 1) != seq_len() - 1).astype(F32)
        hb_ref[0:HALO, :] = (hp_ref[...].astype(F32) * keep_prev).astype(BF16)
        hb_ref[HALO + tm:, :] = (hn_ref[...].astype(F32) * keep_next).astype(BF16)

    rows = row0 + lax.broadcasted_iota(jnp.int32, (tm, 1), 0)
    pos = seq_pos(rows)
    keep_p = (pos != 0).astype(F32)
    keep_n = (pos != seq_len() - 1).astype(F32)
    tot = tm + 2 * HALO

    def conv(w_ref, c_ref, b_ref):
        z = _dot(hb_ref[...], w_ref[...])
        zp = pltpu.roll(z, 1, axis=0)[HALO:HALO + tm]
        zn = pltpu.roll(z, tot - 1, axis=0)[HALO:HALO + tm]
        c = c_ref[...]
        return (zp * keep_p) * c[0:1] + z[HALO:HALO + tm] * c[1:2] + (zn * keep_n) * c[2:3] + b_ref[...]

    gate = conv(wg_ref, cg_ref, bg_ref)
    up = conv(wu_ref, cu_ref, bu_ref)
    o_ref[...] = (gate * _sigmoid(gate) * up).astype(BF16)


def _ffn_up(h, w_all, conv_w_all, conv_b_all, layer, n_ctx, ctx_len, lat_len, tm=1024, tn=512):
    t, d = h.shape
    f = w_all.shape[2] // 2
    nj = f // tn
    hb = tm // HALO
    last_blk = t // HALO - 1
    kern = functools.partial(_ffn_up_kernel, tm=tm, n_ctx=n_ctx, ctx_len=ctx_len, lat_len=lat_len)
    return pl.pallas_call(
        kern,
        grid=(t // tm, nj),
        in_specs=[
            pl.BlockSpec((tm, d), lambda i, j: (i, 0)),
            pl.BlockSpec((HALO, d), lambda i, j: (jnp.maximum(i * hb - 1, 0), 0)),
            pl.BlockSpec((HALO, d), lambda i, j: (jnp.minimum((i + 1) * hb, last_blk), 0)),
            pl.BlockSpec((None, d, tn), lambda i, j: (layer, 0, j)),
            pl.BlockSpec((None, d, tn), lambda i, j: (layer, 0, j + nj)),
            pl.BlockSpec((None, 3, tn), lambda i, j: (layer, 0, j)),
            pl.BlockSpec((None, 3, tn), lambda i, j: (layer, 0, j + nj)),
            pl.BlockSpec((None, 1, tn), lambda i, j: (layer, 0, j)),
            pl.BlockSpec((None, 1, tn), lambda i, j: (layer, 0, j + nj)),
        ],
        out_specs=pl.BlockSpec((tm, tn), lambda i, j: (i, j)),
        out_shape=jax.ShapeDtypeStruct((t, f), BF16),
        scratch_shapes=[pltpu.VMEM((tm + 2 * HALO, d), BF16)],
        compiler_params=_params(("arbitrary", "arbitrary"), 48),
        name="ffn_up",
    )(h, h, h, w_all, w_all, conv_w_all, conv_w_all, conv_b_all, conv_b_all)


def _attend(score_blocks, value_blocks, sink=None):
    m = jnp.max(score_blocks[0], axis=-1, keepdims=True)
    for s in score_blocks[1:]:
        m = jnp.maximum(m, jnp.max(s, axis=-1, keepdims=True))
    if sink is not None:
        m = jnp.maximum(m, sink)
    den = jnp.exp(sink - m) if sink is not None else 0.0
    out = None
    for s, v in zip(score_blocks, value_blocks):
        p = jnp.exp(s - m)
        den = den + jnp.sum(p, axis=-1, keepdims=True)
        o = _dot(p.astype(BF16), v)
        out = o if out is None else out + o
    return out / den


def _scaled_q(q):
    return (q * ATT_SCALE).astype(BF16)


def _ctx_attn_kernel(*refs, n_heads, n_kv, has_sink):
    if has_sink:
        sink_ref, q_ref, k_ref, v_ref, o_ref = refs
    else:
        q_ref, k_ref, v_ref, o_ref = refs
    grp = n_heads // n_kv
    for kv in range(n_kv):
        ksl = slice(kv * HEAD_DIM, (kv + 1) * HEAD_DIM)
        k = k_ref[:, ksl].astype(BF16)
        v = v_ref[:, ksl].astype(BF16)
        for gi in range(grp):
            h = kv * grp + gi
            hsl = slice(h * HEAD_DIM, (h + 1) * HEAD_DIM)
            s = _dot_t(_scaled_q(q_ref[:, hsl]), k)
            sink = sink_ref[h] if has_sink else None
            o_ref[:, hsl] = _attend([s], [v], sink).astype(BF16)


def _ctx_attn(proj, n_seq, seq_len, n_heads, n_kv, sink=None):
    qw = n_heads * HEAD_DIM
    kw = n_kv * HEAD_DIM
    assert qw % kw == 0
    kb = qw // kw
    has_sink = sink is not None
    in_specs = [
        pl.BlockSpec((seq_len, qw), lambda b: (b, 0)),
        pl.BlockSpec((seq_len, kw), lambda b: (b, kb)),
        pl.BlockSpec((seq_len, kw), lambda b: (b, kb + 1)),
    ]
    args = [proj, proj, proj]
    if has_sink:
        in_specs = [pl.BlockSpec(memory_space=pltpu.SMEM)] + in_specs
        args = [sink] + args
    return pl.pallas_call(
        functools.partial(_ctx_attn_kernel, n_heads=n_heads, n_kv=n_kv, has_sink=has_sink),
        grid=(n_seq,),
        in_specs=in_specs,
        out_specs=pl.BlockSpec((seq_len, qw), lambda b: (b, 0)),
        out_shape=jax.ShapeDtypeStruct((n_seq * seq_len, qw), BF16),
        compiler_params=_params(("arbitrary",), 32),
        name="ctx_attn",
    )(*args)


NA_QROWS = 8
NA_KROWS = 16
NA_QBLK = NA_QROWS * GRID_W
NA_KBLK = NA_KROWS * GRID_W
NA_HEADS_PER_STEP = 2


def _na_key_row0(blk, rows):
    return np.clip(blk * NA_QROWS - NA_WIN_R // 2, 0, rows - NA_KROWS)


def _na_row_offsets(rows):
    nblk = rows // NA_QROWS
    a_idx = np.full((3, NA_QROWS, NA_KROWS), -1, np.int32)
    for v, blk in enumerate((0, 1, nblk - 1)):
        kr0 = _na_key_row0(blk, rows)
        for ql in range(NA_QROWS):
            qr = blk * NA_QROWS + ql
            start = np.clip(qr - NA_WIN_R // 2, 0, rows - NA_WIN_R)
            for kl in range(NA_KROWS):
                kr = kr0 + kl
                if start <= kr < start + NA_WIN_R:
                    a_idx[v, ql, kl] = kr - qr + NA_WIN_R - 1
    return a_idx


def _na_bias_kernel(rpb_ref, o_ref, *, a_idx):
    h = pl.program_id(0)
    n_dr = 2 * NA_WIN_R - 1
    n_dc = 2 * NA_WIN_C - 1
    qc = lax.broadcasted_iota(jnp.int32, (GRID_W, GRID_W), 0)
    kc = lax.broadcasted_iota(jnp.int32, (GRID_W, GRID_W), 1)
    col_start = jnp.clip(qc - NA_WIN_C // 2, 0, GRID_W - NA_WIN_C)
    col_ok = (kc >= col_start) & (kc < col_start + NA_WIN_C)
    dc = jnp.clip(kc - qc + NA_WIN_C - 1, 0, n_dc - 1)
    is_dc = [dc == j for j in range(n_dc)]
    masked = jnp.full((GRID_W, GRID_W), NEG_INF, F32)
    tiles = []
    for a in range(n_dr):
        t = jnp.zeros((GRID_W, GRID_W), F32)
        for j in range(n_dc):
            t = jnp.where(is_dc[j], rpb_ref[(h * n_dr + a) * n_dc + j], t)
        tiles.append(jnp.where(col_ok, t, NEG_INF))
    for v in range(a_idx.shape[0]):
        for ql in range(NA_QROWS):
            for kl in range(NA_KROWS):
                a = int(a_idx[v, ql, kl])
                o_ref[v, 0, ql * GRID_W:(ql + 1) * GRID_W, kl * GRID_W:(kl + 1) * GRID_W] = (
                    tiles[a] if a >= 0 else masked)


def _na_bias_tables(rpb, rows):
    h = rpb.shape[0]
    a_idx = _na_row_offsets(rows)
    return pl.pallas_call(
        functools.partial(_na_bias_kernel, a_idx=a_idx),
        grid=(h,),
        in_specs=[pl.BlockSpec(memory_space=pltpu.SMEM)],
        out_specs=pl.BlockSpec((3, 1, NA_QBLK, NA_KBLK), lambda hh: (0, hh, 0, 0)),
        out_shape=jax.ShapeDtypeStruct((3, h, NA_QBLK, NA_KBLK), F32),
        compiler_params=_params(("arbitrary",), 32),
        name="na_bias",
    )(rpb.astype(F32).reshape(-1))


def _na_kernel(q_ref, k_ref, v_ref, kc_ref, vc_ref, bias_ref, o_ref, *, rows):
    i = pl.program_id(2)
    kr0 = jnp.clip(i * NA_QROWS - NA_WIN_R // 2, 0, rows - NA_KROWS)
    k0 = pl.multiple_of(kr0 * GRID_W, GRID_W * (NA_WIN_R // 2))
    for hh in range(NA_HEADS_PER_STEP):
        hsl = slice(hh * HEAD_DIM, (hh + 1) * HEAD_DIM)
        q = _scaled_q(q_ref[:, hsl])
        kw = k_ref[pl.ds(k0, NA_KBLK), hsl].astype(BF16)
        vw = v_ref[pl.ds(k0, NA_KBLK), hsl].astype(BF16)
        s_loc = _dot_t(q, kw) + bias_ref[0, hh]
        s_ctx = _dot_t(q, kc_ref[0, :, hsl].astype(BF16))
        o_ref[:, hsl] = _attend([s_loc, s_ctx], [vw, vc_ref[0, :, hsl].astype(BF16)]).astype(BF16)


def _na_attn(proj, cache_k, cache_v, bias, n_ctx, n_lat, lat_len):
    rows = lat_len // GRID_W
    nblk = lat_len // NA_QBLK
    assert n_ctx % lat_len == 0 and n_ctx % NA_QBLK == 0
    qb0 = n_ctx // NA_QBLK
    sb0 = n_ctx // lat_len
    hps = NA_HEADS_PER_STEP
    hw = hps * HEAD_DIM
    hg = NA_HEADS // hps
    c = cache_k.shape[1]
    variant = lambda i: jnp.where(i == 0, 0, jnp.where(i == nblk - 1, 2, 1))
    return pl.pallas_call(
        functools.partial(_na_kernel, rows=rows),
        grid=(n_lat, hg, nblk),
        in_specs=[
            pl.BlockSpec((NA_QBLK, hw), lambda b, hh, i: (qb0 + b * nblk + i, hh)),
            pl.BlockSpec((lat_len, hw), lambda b, hh, i: (sb0 + b, hg + hh)),
            pl.BlockSpec((lat_len, hw), lambda b, hh, i: (sb0 + b, 2 * hg + hh)),
            pl.BlockSpec((1, c, hw), lambda b, hh, i: (b, 0, hh)),
            pl.BlockSpec((1, c, hw), lambda b, hh, i: (b, 0, hh)),
            pl.BlockSpec((1, hps, NA_QBLK, NA_KBLK), lambda b, hh, i: (variant(i), hh, 0, 0)),
        ],
        out_specs=pl.BlockSpec((NA_QBLK, hw), lambda b, hh, i: (b * nblk + i, hh)),
        out_shape=jax.ShapeDtypeStruct((n_lat * lat_len, NA_HEADS * HEAD_DIM), BF16),
        compiler_params=_params(("arbitrary", "arbitrary", "arbitrary"), 56),
        name="na_attn",
    )(proj, proj, proj, cache_k, cache_v, bias)


SWA_QBLK = 512
SWA_KBLK = SWA_QBLK + 2 * SWA_WINDOW


def _rope_tables(length):
    half = HEAD_DIM // 2
    freqs = ROPE_BASE ** (-jnp.arange(0, half, 2, dtype=F32) / half)
    t = jnp.arange(length)
    ang_r = (t // GRID_W).astype(F32)[:, None] * freqs[None, :]
    ang_c = (t % GRID_W).astype(F32)[:, None] * freqs[None, :]
    ang = jnp.concatenate([ang_r, ang_r, ang_c, ang_c], axis=1)
    sign = np.concatenate([-np.ones(half // 2), np.ones(half // 2)] * 2).astype(np.float32)
    return jnp.cos(ang), jnp.sin(ang) * jnp.asarray(sign)[None, :]


def _rope(x, cos, sin_signed):
    lane = lax.broadcasted_iota(jnp.int32, x.shape, 1)
    quarter = HEAD_DIM // 4
    lower = (lane & (2 * quarter - 1)) < quarter
    swapped = jnp.where(lower, pltpu.roll(x, HEAD_DIM - quarter, axis=1), pltpu.roll(x, quarter, axis=1))
    return x * cos + swapped * sin_signed


def _swa_kernel(sink_ref, q_ref, k_ref, v_ref, kc_ref, vc_ref, cos_ref, sin_ref, o_ref, *, lat_len):
    kv = pl.program_id(1)
    i = pl.program_id(2)
    grp = SWA_HEADS // SWA_KV_HEADS
    q0 = pl.multiple_of(i * SWA_QBLK, SWA_QBLK)
    k0 = pl.multiple_of(jnp.clip(i * SWA_QBLK - SWA_WINDOW, 0, lat_len - SWA_KBLK), SWA_WINDOW)
    kw = _rope(k_ref[pl.ds(k0, SWA_KBLK), :], cos_ref[pl.ds(k0, SWA_KBLK), :],
               sin_ref[pl.ds(k0, SWA_KBLK), :]).astype(BF16)
    vw = v_ref[pl.ds(k0, SWA_KBLK), :].astype(BF16)
    kc = kc_ref[0].astype(BF16)
    vc = vc_ref[0].astype(BF16)
    qpos = q0 + lax.broadcasted_iota(jnp.int32, (SWA_QBLK, SWA_KBLK), 0)
    kpos = k0 + lax.broadcasted_iota(jnp.int32, (SWA_QBLK, SWA_KBLK), 1)
    band = jnp.where(jnp.abs(qpos - kpos) <= SWA_WINDOW, 0.0, NEG_INF)
    cos_q = cos_ref[pl.ds(q0, SWA_QBLK), :]
    sin_q = sin_ref[pl.ds(q0, SWA_QBLK), :]
    for gi in range(grp):
        hsl = slice(gi * HEAD_DIM, (gi + 1) * HEAD_DIM)
        q = _scaled_q(_rope(q_ref[:, hsl], cos_q, sin_q))
        s_loc = _dot_t(q, kw) + band
        s_ctx = _dot_t(q, kc)
        sink = sink_ref[kv * grp + gi]
        o_ref[:, hsl] = _attend([s_loc, s_ctx], [vw, vc], sink).astype(BF16)


def _swa_attn(proj, cache_k, cache_v, sink, cos, sin_signed, n_ctx, n_lat, lat_len):
    grp = SWA_HEADS // SWA_KV_HEADS
    gw = grp * HEAD_DIM
    nblk = lat_len // SWA_QBLK
    qb0 = n_ctx // SWA_QBLK
    sb0 = n_ctx // lat_len
    kcol0 = SWA_HEADS
    vcol0 = SWA_HEADS + SWA_KV_HEADS
    c = cache_k.shape[1]
    return pl.pallas_call(
        functools.partial(_swa_kernel, lat_len=lat_len),
        grid=(n_lat, SWA_KV_HEADS, nblk),
        in_specs=[
            pl.BlockSpec(memory_space=pltpu.SMEM),
            pl.BlockSpec((SWA_QBLK, gw), lambda b, kv, i: (qb0 + b * nblk + i, kv)),
            pl.BlockSpec((lat_len, HEAD_DIM), lambda b, kv, i: (sb0 + b, kcol0 + kv)),
            pl.BlockSpec((lat_len, HEAD_DIM), lambda b, kv, i: (sb0 + b, vcol0 + kv)),
            pl.BlockSpec((1, c, HEAD_DIM), lambda b, kv, i: (b, 0, kv)),
            pl.BlockSpec((1, c, HEAD_DIM), lambda b, kv, i: (b, 0, kv)),
            pl.BlockSpec((lat_len, HEAD_DIM), lambda b, kv, i: (0, 0)),
            pl.BlockSpec((lat_len, HEAD_DIM), lambda b, kv, i: (0, 0)),
        ],
        out_specs=pl.BlockSpec((SWA_QBLK, gw), lambda b, kv, i: (b * nblk + i, kv)),
        out_shape=jax.ShapeDtypeStruct((n_lat * lat_len, SWA_HEADS * HEAD_DIM), BF16),
        compiler_params=_params(("arbitrary", "arbitrary", "arbitrary"), 48),
        name="swa_attn",
    )(sink, proj, proj, proj, cache_k, cache_v, cos, sin_signed)


def _s5_disc_kernel(are_ref, aim_ref, ldt_ref, bre_ref, bim_ref, abre_ref, abim_ref, bbre_ref, bbim_ref):
    a_re = are_ref[...]
    a_im = aim_ref[...]
    dt = jnp.exp(ldt_ref[...])
    mag = jnp.exp(a_re * dt)
    ang = a_im * dt
    ab_re = mag * jnp.cos(ang)
    ab_im = mag * jnp.sin(ang)
    den = a_re * a_re + a_im * a_im
    n_re = ab_re - 1.0
    f_re = (n_re * a_re + ab_im * a_im) / den
    f_im = (ab_im * a_re - n_re * a_im) / den
    b_re = bre_ref[...]
    b_im = bim_ref[...]
    abre_ref[...] = ab_re
    abim_ref[...] = ab_im
    bbre_ref[...] = f_re * b_re - f_im * b_im
    bbim_ref[...] = f_re * b_im + f_im * b_re


def _s5_discretise(a_re, a_im, log_dt, b_re, b_im):
    nd, g, p, c = b_re.shape
    shape2 = (nd * g, p * c)
    expand = lambda a: jnp.broadcast_to(a[..., None], (nd, g, p, c)).reshape(shape2)
    ldt = jnp.broadcast_to(log_dt[:, :, None, None], (nd, g, p, c)).reshape(shape2)
    out = pl.pallas_call(
        _s5_disc_kernel,
        out_shape=[jax.ShapeDtypeStruct(shape2, F32)] * 4,
        compiler_params=_params(None, 16),
        name="s5_discretise",
    )(expand(a_re), expand(a_im), ldt, b_re.reshape(shape2), b_im.reshape(shape2))
    ab_re, ab_im, bb_re, bb_im = [o.reshape(nd, g, p, c) for o in out]
    return ab_re[..., 0], ab_im[..., 0], bb_re, bb_im


def _block_diag(m, gb):
    g, a, b = m.shape
    m = m.reshape(g // gb, gb, a, b)
    eye = jnp.eye(gb, dtype=m.dtype)
    out = m[:, :, :, None, :] * eye[None, :, None, :, None]
    return out.reshape(g // gb, gb * a, gb * b)


def _cmul_add(ar, ai, xr, xi, br, bi):
    return ar * xr - ai * xi + br, ar * xi + ai * xr + bi


def _s5_kernel(*refs, latent):
    if latent:
        (u_ref, bbr_ref, bbi_ref, ccr_ref, cci_ref, ar_ref, ai_ref, d_ref, h0_ref,
         y_ref, xr_s, xi_s, cr_s, ci_s) = refs
    else:
        (u_ref, bbr_ref, bbi_ref, ccr_ref, cci_ref, ar_ref, ai_ref, d_ref,
         y_ref, fin_ref, xr_s, xi_s) = refs
    rows, steps = S5_ROWS, S5_STEPS
    u = u_ref[0]
    ub = u.astype(BF16)
    y = d_ref[0] * u
    for dirn in range(2):
        reverse = dirn == 1
        xr_s[...] = _dot(ub, bbr_ref[dirn, 0])
        xi_s[...] = _dot(ub, bbi_ref[dirn, 0])
        a_r1 = ar_ref[dirn, 0]
        a_i1 = ai_ref[dirn, 0]
        a_r = jnp.broadcast_to(a_r1, (rows, S5_GB_ST))
        a_i = jnp.broadcast_to(a_i1, (rows, S5_GB_ST))

        def time_rows(s):
            t = (steps - 1 - s) if reverse else s
            if isinstance(t, int):
                return pl.ds(t * rows, rows)
            return pl.ds(pl.multiple_of(t * rows, rows), rows)

        first = time_rows(0)
        final = time_rows(steps - 1)

        def scan_body(s, carry):
            pr, pi = carry
            sl = time_rows(s)
            nr, ni = _cmul_add(a_r, a_i, pr, pi, xr_s[sl, :], xi_s[sl, :])
            xr_s[sl, :] = nr
            xi_s[sl, :] = ni
            return nr, ni

        lax.fori_loop(1, steps, scan_body, (xr_s[first, :], xi_s[first, :]), unroll=4)

        if not latent:
            fin_ref[0, dirn, 0] = xr_s[final, :]
            fin_ref[0, dirn, 1] = xi_s[final, :]
        else:
            p_r, p_i = a_r1, a_i1
            for _ in range(int(math.log2(steps))):
                p_r, p_i = p_r * p_r - p_i * p_i, 2.0 * p_r * p_i
            assert 1 << int(math.log2(steps)) == steps
            e_r = xr_s[final, :]
            e_i = xi_s[final, :]
            c_r = h0_ref[0, 2 * dirn]
            c_i = h0_ref[0, 2 * dirn + 1]
            order = range(rows - 1, -1, -1) if reverse else range(rows)
            for j in order:
                cr_s[j:j + 1, :] = c_r
                ci_s[j:j + 1, :] = c_i
                c_r, c_i = _cmul_add(p_r, p_i, c_r, c_i, e_r[j:j + 1], e_i[j:j + 1])
            zero = jnp.zeros((rows, S5_GB_ST), F32)
            d_r, d_i = _cmul_add(a_r, a_i, cr_s[...], ci_s[...], zero, zero)

            def fix_body(s, carry):
                dr, di = carry
                sl = time_rows(s)
                xr_s[sl, :] = xr_s[sl, :] + dr
                xi_s[sl, :] = xi_s[sl, :] + di
                return _cmul_add(a_r, a_i, dr, di, zero, zero)

            lax.fori_loop(0, steps, fix_body, (d_r, d_i), unroll=4)

        y = y + _dot(xr_s[...].astype(BF16), ccr_ref[dirn, 0]) - _dot(xi_s[...].astype(BF16), cci_ref[dirn, 0])
    y_ref[0] = y


def _s5_scan(u_tm, tile0, tiles, bbr, bbi, ccr, cci, ab_re, ab_im, d, h0=None):
    _, tr, w = u_tm.shape
    assert tr == S5_STEPS * S5_ROWS
    nb = w // S5_GB_IN
    latent = h0 is not None
    in_specs = [
        pl.BlockSpec((1, tr, S5_GB_IN), lambda t, gb: (tile0 + t, 0, gb)),
        pl.BlockSpec((2, 1, S5_GB_IN, S5_GB_ST), lambda t, gb: (0, gb, 0, 0)),
        pl.BlockSpec((2, 1, S5_GB_IN, S5_GB_ST), lambda t, gb: (0, gb, 0, 0)),
        pl.BlockSpec((2, 1, S5_GB_ST, S5_GB_IN), lambda t, gb: (0, gb, 0, 0)),
        pl.BlockSpec((2, 1, S5_GB_ST, S5_GB_IN), lambda t, gb: (0, gb, 0, 0)),
        pl.BlockSpec((2, 1, 1, S5_GB_ST), lambda t, gb: (0, gb, 0, 0)),
        pl.BlockSpec((2, 1, 1, S5_GB_ST), lambda t, gb: (0, gb, 0, 0)),
        pl.BlockSpec((1, 1, S5_GB_IN), lambda t, gb: (gb, 0, 0)),
    ]
    args = [u_tm, bbr, bbi, ccr, cci, ab_re, ab_im, d]
    y_spec = pl.BlockSpec((1, tr, S5_GB_IN), lambda t, gb: (t, 0, gb))
    y_shape = jax.ShapeDtypeStruct((tiles, tr, w), F32)
    scratch = [pltpu.VMEM((tr, S5_GB_ST), F32), pltpu.VMEM((tr, S5_GB_ST), F32)]
    if latent:
        in_specs.append(pl.BlockSpec((1, 4, 1, S5_GB_ST), lambda t, gb: (t, 0, 0, gb)))
        args.append(h0)
        out_specs, out_shape = y_spec, y_shape
        scratch += [pltpu.VMEM((S5_ROWS, S5_GB_ST), F32), pltpu.VMEM((S5_ROWS, S5_GB_ST), F32)]
    else:
        out_specs = [y_spec, pl.BlockSpec((1, 2, 2, S5_ROWS, S5_GB_ST), lambda t, gb: (t, 0, 0, 0, gb))]
        out_shape = [y_shape, jax.ShapeDtypeStruct((tiles, 2, 2, S5_ROWS, nb * S5_GB_ST), F32)]
    return pl.pallas_call(
        functools.partial(_s5_kernel, latent=latent),
        grid=(tiles, nb),
        in_specs=in_specs,
        out_specs=out_specs,
        out_shape=out_shape,
        scratch_shapes=scratch,
        compiler_params=_params(("arbitrary", "arbitrary"), 48),
        name="s5_scan_lat" if latent else "s5_scan_ctx",
    )(*args)


def _glu_kernel(*refs, n_y, first_tiles):
    y_refs = refs[:n_y]
    w_ref, b_ref, o_ref = refs[n_y:]

    def run(y_ref):
        y = y_ref[...]
        g = 0.5 * y * (1.0 + jnp.tanh(math.sqrt(2.0 / math.pi) * (y + 0.044715 * (y * y * y))))
        z = _dot(g.astype(BF16), w_ref[...]) + b_ref[...]
        o_ref[...] = (g * _sigmoid(z)).astype(BF16)

    _select_rows(y_refs, pl.program_id(0), first_tiles, run)


def _glu(y, w, b, tm=1024):
    t, n = y.rows, y.width
    return pl.pallas_call(
        functools.partial(_glu_kernel, n_y=y.n_refs(), first_tiles=y.first_tiles(tm)),
        grid=(t // tm,),
        in_specs=y.specs(tm) + [
            pl.BlockSpec((n, n), lambda i: (0, 0)),
            pl.BlockSpec((1, n), lambda i: (0, 0)),
        ],
        out_specs=pl.BlockSpec((tm, n), lambda i: (i, 0)),
        out_shape=jax.ShapeDtypeStruct((t, n), BF16),
        compiler_params=_params(("arbitrary",), 32),
        name="s5_glu",
    )(*y.arrays, w, b.reshape(1, n))


def _s5_mixer(u, p, state, n_ctx_seq, ctx_len, n_lat, lat_len):
    t, w = u.shape
    g = w // S5_GROUP
    assert ctx_len == S5_STEPS and lat_len == S5_STEPS * S5_ROWS and n_ctx_seq % S5_ROWS == 0
    ab_re, ab_im, bb_re, bb_im = _s5_discretise(p["a_re"], p["a_im"], p["log_dt"], p["b_re"], p["b_im"])
    nb = g // S5_GB
    blocks = lambda m: jnp.stack([_block_diag(m[d].transpose(0, 2, 1), S5_GB) for d in range(2)]).astype(BF16)
    bbr, bbi = blocks(bb_re), blocks(bb_im)
    ccr, cci = blocks(p["c_re"].astype(F32)), blocks(p["c_im"].astype(F32))
    a_r = ab_re.reshape(2, nb, 1, S5_GB_ST)
    a_i = ab_im.reshape(2, nb, 1, S5_GB_ST)
    d = p["d"].astype(F32).reshape(nb, 1, S5_GB_IN)
    tiles = t // (S5_ROWS * S5_STEPS)
    ctx_tiles = n_ctx_seq // S5_ROWS
    tr = S5_STEPS * S5_ROWS
    u_tm = u.reshape(tiles, S5_ROWS, S5_STEPS, w).transpose(0, 2, 1, 3).reshape(tiles, tr, w)
    y_ctx, fin = _s5_scan(u_tm, 0, ctx_tiles, bbr, bbi, ccr, cci, a_r, a_i, d)
    h0 = state.astype(F32).transpose(0, 1, 4, 2, 3).reshape(n_lat, 4, 1, g * S5_STATE)
    y_lat = _s5_scan(u_tm, ctx_tiles, tiles - ctx_tiles, bbr, bbi, ccr, cci, a_r, a_i, d, h0=h0)
    y = _Rows(y_ctx.reshape(ctx_tiles * tr, w), y_lat.reshape((tiles - ctx_tiles) * tr, w))
    o = _glu(y, p["glu_w"].astype(BF16), p["glu_b"].astype(F32))
    o = o.reshape(tiles, S5_STEPS, S5_ROWS, w).transpose(0, 2, 1, 3).reshape(t, w)
    fin = fin.transpose(0, 3, 1, 4, 2).reshape(n_ctx_seq, 2, g, S5_STATE, 2)
    return o, fin


def kernel(x_prompt, x_sample, c, cache_na_k, cache_na_v, state_s5, cache_swa_k, cache_swa_v, c_ctx, norm_mix, norm_ffn, ada_w, ada_b, ab_w_in, ab_w_out, na_rpb, s5_a_re, s5_a_im, s5_log_dt, s5_b_re, s5_b_im, s5_c_re, s5_c_im, s5_d, s5_glu_w, s5_glu_b, swa_w_in, swa_w_out, swa_sink, ffn_w_up, ffn_conv_w, ffn_conv_b, ffn_w_down, final_norm):
    bp, lp, d = x_prompt.shape
    bs, ls, _ = x_sample.shape
    depth = ada_w.shape[0]
    n_ctx = bp * lp
    n_lat = bs * ls
    na_w = NA_HEADS * HEAD_DIM
    swa_q = SWA_HEADS * HEAD_DIM
    swa_kv = SWA_KV_HEADS * HEAD_DIM
    assert depth == 2, "layer pattern below is written for one A/B layer followed by one C layer"

    cond = jnp.concatenate([c_ctx[None, :], c, jnp.zeros((SUBLANES - 1 - bs, d), F32)], axis=0)
    mods = _adaln(cond, ada_w, ada_b).reshape(depth, SUBLANES, 6, 1, d)
    cos, sin_signed = _rope_tables(ls)
    w_up = ffn_w_up.astype(BF16)
    w_down = ffn_w_down.astype(BF16)
    conv_b = ffn_conv_b.reshape(depth, 1, -1)

    x0 = _Rows(x_prompt.reshape(n_ctx, d), x_sample.reshape(n_lat, d))
    proj = _proj(_norm_mod(x0, mods[0], norm_mix[0], n_ctx, ls), ab_w_in[0].astype(BF16))
    att_ctx = _ctx_attn(proj, bp, lp, NA_HEADS, NA_HEADS)
    bias = _na_bias_tables(na_rpb[0], ls // GRID_W)
    att_lat = _na_attn(proj, cache_na_k[:, 0].reshape(bs, -1, na_w), cache_na_v[:, 0].reshape(bs, -1, na_w),
                       bias, n_ctx, bs, ls)
    s5p = dict(a_re=s5_a_re[0], a_im=s5_a_im[0], log_dt=s5_log_dt[0], b_re=s5_b_re[0], b_im=s5_b_im[0],
               c_re=s5_c_re[0], c_im=s5_c_im[0], d=s5_d[0], glu_w=s5_glu_w[0], glu_b=s5_glu_b[0])
    s5o, s5_fin = _s5_mixer(proj[:, 3 * na_w:], s5p, state_s5[:, 0], bp, lp, bs, ls)
    x, h = _outproj([_Rows(att_ctx, att_lat), _Rows(s5o)], ab_w_out[0].astype(BF16), x0, mods[0], norm_ffn[0],
                    n_ctx, ls)
    na_k = proj[:n_ctx, na_w:2 * na_w].reshape(bp, 1, lp, NA_HEADS, HEAD_DIM)
    na_v = proj[:n_ctx, 2 * na_w:3 * na_w].reshape(bp, 1, lp, NA_HEADS, HEAD_DIM)
    act = _ffn_up(h, w_up, ffn_conv_w, conv_b, 0, n_ctx, lp, ls)
    x, h = _ffn_down(act, w_down, 0, x, mods[0], mods[1], norm_mix[1], n_ctx, ls)

    proj = _proj(h, swa_w_in[0].astype(BF16))
    sink = swa_sink[0].astype(F32)
    att_ctx = _ctx_attn(proj, bp, lp, SWA_HEADS, SWA_KV_HEADS, sink=sink)
    att_lat = _swa_attn(proj, cache_swa_k[:, 0].reshape(bs, -1, swa_kv), cache_swa_v[:, 0].reshape(bs, -1, swa_kv),
                        sink, cos, sin_signed, n_ctx, bs, ls)
    x, h = _outproj([_Rows(att_ctx, att_lat)], swa_w_out[0].astype(BF16), _Rows(x), mods[1], norm_ffn[1], n_ctx, ls)
    swa_k = proj[:n_ctx, swa_q:swa_q + swa_kv].reshape(bp, 1, lp, SWA_KV_HEADS, HEAD_DIM)
    swa_v = proj[:n_ctx, swa_q + swa_kv:].reshape(bp, 1, lp, SWA_KV_HEADS, HEAD_DIM)
    act = _ffn_up(h, w_up, ffn_conv_w, conv_b, 1, n_ctx, lp, ls)
    y_p = _ffn_down(act, w_down, 1, x, mods[1], None, final_norm, n_ctx, ls, row0=0, n_rows=n_ctx)
    y_s = _ffn_down(act, w_down, 1, x, mods[1], None, final_norm, n_ctx, ls, row0=n_ctx, n_rows=n_lat)

    return (y_p.reshape(bp, lp, d), y_s.reshape(bs, ls, d), na_k, na_v, s5_fin[:, None], swa_k, swa_v)
```

```python
import functools
import math

import jax
import jax.numpy as jnp
import numpy as np
from jax import lax
from jax.experimental import pallas as pl
from jax.experimental.pallas import tpu as pltpu

F32 = jnp.float32
BF16 = jnp.bfloat16

GRID_W = 64
HEAD_DIM = 128
NA_HEADS = 8
NA_WIN_R = 8
NA_WIN_C = 16
S5_GROUP = 16
S5_STATE = 64
SWA_HEADS = 16
SWA_KV_HEADS = 4
SWA_WINDOW = 128
ROPE_BASE = 10000.0
EPS = 1e-6
NEG_INF = -1e30
ATT_SCALE = HEAD_DIM ** -0.5

VMEM_PHYSICAL_MIB = 64
SUBLANES = 8
BF16_SUBLANES = 16
LANES = 128

S5_STEPS = 256
S5_ROWS = 16
S5_GB = 8
S5_GB_IN = S5_GB * S5_GROUP
S5_GB_ST = S5_GB * S5_STATE


def _params(sem, vmem_mib):
    assert vmem_mib <= VMEM_PHYSICAL_MIB
    if sem is None:
        return pltpu.CompilerParams(vmem_limit_bytes=vmem_mib << 20)
    return pltpu.CompilerParams(dimension_semantics=sem, vmem_limit_bytes=vmem_mib << 20)


def _dot(a, b):
    return jnp.dot(a, b, preferred_element_type=F32)


def _dot_t(a, b):
    return lax.dot_general(a, b, (((1,), (1,)), ((), ())), preferred_element_type=F32)


def _sigmoid(x):
    return 1.0 / (1.0 + jnp.exp(-x))


class _Rows:
    def __init__(self, *arrays):
        assert len(arrays) in (1, 2)
        self.arrays = arrays
        self.width = arrays[0].shape[1]
        self.dtype = arrays[0].dtype
        self.rows = sum(a.shape[0] for a in arrays)

    def specs(self, tm, row_of_grid=lambda *g: g[0]):
        if len(self.arrays) == 1:
            return [pl.BlockSpec((tm, self.width), lambda *g: (row_of_grid(*g), 0))]
        na = self.arrays[0].shape[0] // tm
        return [
            pl.BlockSpec((tm, self.width), lambda *g: (jnp.minimum(row_of_grid(*g), na - 1), 0)),
            pl.BlockSpec((tm, self.width), lambda *g: (jnp.maximum(row_of_grid(*g) - na, 0), 0)),
        ]

    def n_refs(self):
        return len(self.arrays)

    def first_tiles(self, tm):
        return self.arrays[0].shape[0] // tm


def _select_rows(refs, tile, first_tiles, consume):
    if len(refs) == 1:
        consume(refs[0])
        return

    @pl.when(tile < first_tiles)
    def _():
        consume(refs[0])

    @pl.when(tile >= first_tiles)
    def _():
        consume(refs[1])


def _staged_rows(refs, tile, first_tiles, stage_ref):
    if len(refs) == 1:
        return refs[0]

    def copy(ref):
        stage_ref[...] = ref[...]
    _select_rows(refs, tile, first_tiles, copy)
    return stage_ref


def _adaln_kernel(c_ref, w_ref, b_ref, o_ref):
    c = c_ref[...]
    s = (c * _sigmoid(c)).astype(BF16)
    o_ref[0] = _dot(s, w_ref[0].astype(BF16)) + b_ref[0]


def _adaln(cond8, ada_w, ada_b):
    depth, d, n = ada_w.shape
    tn = 1024
    return pl.pallas_call(
        _adaln_kernel,
        grid=(depth, n // tn),
        in_specs=[
            pl.BlockSpec((SUBLANES, d), lambda l, j: (0, 0)),
            pl.BlockSpec((1, d, tn), lambda l, j: (l, 0, j)),
            pl.BlockSpec((1, 1, tn), lambda l, j: (l, 0, j)),
        ],
        out_specs=pl.BlockSpec((1, SUBLANES, tn), lambda l, j: (l, 0, j)),
        out_shape=jax.ShapeDtypeStruct((depth, SUBLANES, n), F32),
        compiler_params=_params(("arbitrary", "arbitrary"), 40),
        name="adaln",
    )(cond8, ada_w, ada_b.reshape(depth, 1, n))


MOD_SHIFT1, MOD_SCALE1, MOD_GATE1, MOD_SHIFT2, MOD_SCALE2, MOD_GATE2 = range(6)


def _group_of(row0, n_ctx, lat_len):
    return jnp.maximum((row0 - n_ctx) // lat_len + 1, 0)


def _mod_spec(which, tm, n_ctx, lat_len, d):
    return pl.BlockSpec((1, 1, 1, d), lambda *g: (_group_of(g[0] * tm, n_ctx, lat_len), which, 0, 0))


def _modulated(x, g, shift, scale):
    ms = jnp.mean(x * x, axis=-1, keepdims=True)
    y = x * lax.rsqrt(ms + EPS) * g
    return y * (1.0 + scale) + shift


def _norm_mod_kernel(*refs, n_x, first_tiles):
    x_refs = refs[:n_x]
    sh_ref, sc_ref, g_ref, o_ref = refs[n_x:]

    def run(x_ref):
        o_ref[...] = _modulated(x_ref[...], g_ref[...], sh_ref[0, 0], sc_ref[0, 0]).astype(BF16)

    _select_rows(x_refs, pl.program_id(0), first_tiles, run)


def _norm_mod(x, mod, g, n_ctx, lat_len, tm=512):
    t, d = x.rows, x.width
    return pl.pallas_call(
        functools.partial(_norm_mod_kernel, n_x=x.n_refs(), first_tiles=x.first_tiles(tm)),
        grid=(t // tm,),
        in_specs=x.specs(tm) + [
            _mod_spec(MOD_SHIFT1, tm, n_ctx, lat_len, d),
            _mod_spec(MOD_SCALE1, tm, n_ctx, lat_len, d),
            pl.BlockSpec((1, d), lambda i: (0, 0)),
        ],
        out_specs=pl.BlockSpec((tm, d), lambda i: (i, 0)),
        out_shape=jax.ShapeDtypeStruct((t, d), BF16),
        compiler_params=_params(("arbitrary",), 40),
        name="norm_mod",
    )(*x.arrays, mod, mod, g.reshape(1, d))


class _ProjOut:
    def __init__(self, j0, j1, part, dtype, scale=None, cols=None):
        assert part in ("all", "ctx", "lat") and (cols is None or j1 == j0 + 1)
        self.j0, self.j1, self.part, self.dtype, self.scale, self.cols = j0, j1, part, dtype, scale, cols


def _proj_kernel(h_ref, w_ref, *o_refs, outs, ctx_tiles):
    i = pl.program_id(0)
    j = pl.program_id(1)
    for o, o_ref in zip(outs, o_refs):
        mine = (j >= o.j0) & (j < o.j1)
        if o.part == "ctx":
            mine = mine & (i < ctx_tiles)
        elif o.part == "lat":
            mine = mine & (i >= ctx_tiles)

        @pl.when(mine)
        def _(o=o, o_ref=o_ref):
            w = w_ref[...] if o.cols is None else w_ref[:, o.cols[0]:o.cols[1]]
            v = _dot(h_ref[...], w)
            if o.scale is not None:
                v = v * o.scale
            o_ref[...] = v.astype(o.dtype)


def _proj(h, w, outs, n_ctx, tm=1024, tn=1024):
    t, d = h.shape
    n = w.shape[1]
    nc = n_ctx // tm
    n_rows = {"all": t, "ctx": n_ctx, "lat": t - n_ctx}

    def block_index(o, tiles, i, j):
        col = jnp.clip(j - o.j0, 0, tiles - 1)
        if o.part == "ctx":
            return jnp.minimum(i, nc - 1), jnp.where(i < nc, col, tiles - 1)
        if o.part == "lat":
            return jnp.maximum(i - nc, 0), jnp.where(i >= nc, col, 0)
        return i, col

    out_specs, out_shape = [], []
    for o in outs:
        width = tn if o.cols is None else o.cols[1] - o.cols[0]
        tiles = o.j1 - o.j0
        out_specs.append(pl.BlockSpec((tm, width), functools.partial(block_index, o, tiles)))
        out_shape.append(jax.ShapeDtypeStruct((n_rows[o.part], width * tiles), o.dtype))
    return pl.pallas_call(
        functools.partial(_proj_kernel, outs=outs, ctx_tiles=nc),
        grid=(t // tm, n // tn),
        in_specs=[pl.BlockSpec((tm, d), lambda i, j: (i, 0)), pl.BlockSpec((d, tn), lambda i, j: (0, j))],
        out_specs=out_specs,
        out_shape=out_shape,
        compiler_params=_params(("arbitrary", "arbitrary"), 60),
        name="proj",
    )(h, w)


def _to_bf16_kernel(w_ref, o_ref):
    o_ref[...] = w_ref[...].astype(BF16)


def _to_bf16(w, block_bytes=8 << 20):
    shape = w.shape
    w2 = w.reshape(-1, shape[-1])
    r, c = w2.shape
    tr = max(BF16_SUBLANES, min(r, (block_bytes // (4 * c)) // BF16_SUBLANES * BF16_SUBLANES))
    while r % tr:
        tr -= BF16_SUBLANES
    out = pl.pallas_call(
        _to_bf16_kernel,
        grid=(r // tr,),
        in_specs=[pl.BlockSpec((tr, c), lambda i: (i, 0))],
        out_specs=pl.BlockSpec((tr, c), lambda i: (i, 0)),
        out_shape=jax.ShapeDtypeStruct((r, c), BF16),
        compiler_params=_params(("arbitrary",), 40),
        name="to_bf16",
    )(w2)
    return out.reshape(shape)


COL_CHUNK = 512


def _residual_norm(acc_of_cols, x_of_cols, gate_ref, xo_ref, d, tm):
    ss = jnp.zeros((tm, 1), F32)
    for c0 in range(0, d, COL_CHUNK):
        cols = slice(c0, c0 + COL_CHUNK)
        xn = x_of_cols(cols) + gate_ref[0, 0, :, cols] * acc_of_cols(cols)
        xo_ref[:, cols] = xn
        ss = ss + jnp.sum(xn * xn, axis=-1, keepdims=True)
    return lax.rsqrt(ss * (1.0 / d) + EPS)


def _write_modulated(xo_ref, inv, g_ref, sh_ref, sc_ref, h_ref, d):
    for c0 in range(0, d, COL_CHUNK):
        cols = slice(c0, c0 + COL_CHUNK)
        y = xo_ref[:, cols] * inv * g_ref[:, cols]
        h_ref[:, cols] = (y * (1.0 + sc_ref[0, 0, :, cols]) + sh_ref[0, 0, :, cols]).astype(BF16)


def _outproj_kernel(*refs, part_refs, part_first, n_x, x_first, d, tm):
    pos = 0
    lhs_refs = []
    for n in part_refs:
        lhs_refs.append(refs[pos:pos + n])
        pos += n
    x_refs = refs[pos:pos + n_x]
    pos += n_x
    w_ref, gate_ref, g_ref, sh_ref, sc_ref, xo_ref, h_ref = refs[pos:pos + 7]
    stage = list(refs[pos + 7:])
    tile = pl.program_id(0)

    lhs = []
    for prefs, first in zip(lhs_refs, part_first):
        lhs.append(_staged_rows(prefs, tile, first, stage.pop(0) if len(prefs) > 1 else None))
    x_ref = _staged_rows(x_refs, tile, x_first, stage.pop(0) if n_x > 1 else None)

    def acc_of_cols(cols):
        row = 0
        acc = None
        for l_ref in lhs:
            k = l_ref.shape[1]
            part = _dot(l_ref[...], w_ref[row:row + k, cols])
            acc = part if acc is None else acc + part
            row += k
        return acc

    inv = _residual_norm(acc_of_cols, lambda cols: x_ref[:, cols], gate_ref, xo_ref, d, tm)
    _write_modulated(xo_ref, inv, g_ref, sh_ref, sc_ref, h_ref, d)


def _outproj(parts, w, x, mod, g_next, n_ctx, lat_len, tm=512):
    t, d = x.rows, x.width
    in_specs, args, scratch = [], [], []
    for p in parts:
        in_specs += p.specs(tm)
        args += list(p.arrays)
    in_specs += x.specs(tm)
    args += list(x.arrays)
    in_specs += [
        pl.BlockSpec(w.shape, lambda i: (0, 0), pipeline_mode=pl.Buffered(1)),
        _mod_spec(MOD_GATE1, tm, n_ctx, lat_len, d),
        pl.BlockSpec((1, d), lambda i: (0, 0)),
        _mod_spec(MOD_SHIFT2, tm, n_ctx, lat_len, d),
        _mod_spec(MOD_SCALE2, tm, n_ctx, lat_len, d),
    ]
    args += [w, mod, g_next.reshape(1, d), mod, mod]
    scratch = [pltpu.VMEM((tm, p.width), p.dtype) for p in parts if p.n_refs() > 1]
    if x.n_refs() > 1:
        scratch.append(pltpu.VMEM((tm, d), F32))
    kern = functools.partial(
        _outproj_kernel, part_refs=[p.n_refs() for p in parts], part_first=[p.first_tiles(tm) for p in parts],
        n_x=x.n_refs(), x_first=x.first_tiles(tm), d=d, tm=tm)
    return pl.pallas_call(
        kern,
        grid=(t // tm,),
        in_specs=in_specs,
        out_specs=[pl.BlockSpec((tm, d), lambda i: (i, 0)), pl.BlockSpec((tm, d), lambda i: (i, 0))],
        out_shape=[jax.ShapeDtypeStruct((t, d), F32), jax.ShapeDtypeStruct((t, d), BF16)],
        scratch_shapes=scratch,
        compiler_params=_params(("arbitrary",), 56),
        name="outproj",
    )(*args)


def _ffn_down_kernel(act_ref, w_ref, x_ref, gate_ref, g_ref, *rest, final, d, tm, n_k):
    if final:
        (acc_ref,) = rest
    else:
        sh_ref, sc_ref, acc_ref, h_ref = rest
    k = pl.program_id(1)

    @pl.when(k == 0)
    def _():
        for c0 in range(0, d, COL_CHUNK):
            acc_ref[:, c0:c0 + COL_CHUNK] = _dot(act_ref[...], w_ref[:, c0:c0 + COL_CHUNK])

    @pl.when(k > 0)
    def _():
        for c0 in range(0, d, COL_CHUNK):
            acc_ref[:, c0:c0 + COL_CHUNK] += _dot(act_ref[...], w_ref[:, c0:c0 + COL_CHUNK])

    @pl.when(k == n_k - 1)
    def _():
        inv = _residual_norm(lambda cols: acc_ref[:, cols], lambda cols: x_ref[:, cols], gate_ref, acc_ref, d, tm)
        if final:
            for c0 in range(0, d, COL_CHUNK):
                cols = slice(c0, c0 + COL_CHUNK)
                acc_ref[:, cols] = acc_ref[:, cols] * inv * g_ref[:, cols]
        else:
            _write_modulated(acc_ref, inv, g_ref, sh_ref, sc_ref, h_ref, d)


def _ffn_down(act, w_all, layer, x, mod, mod_next, g_next, n_ctx, lat_len, row0=0, n_rows=None, tm=1024):
    t, d = x.shape
    f = act.shape[1]
    final = mod_next is None
    n_rows = t if n_rows is None else n_rows
    tk = f // 4
    assert f % tk == 0 and tk % LANES == 0 and row0 % tm == 0 and n_rows % tm == 0
    n_k = f // tk
    t0 = row0 // tm
    mod_spec = lambda which: pl.BlockSpec(
        (1, 1, 1, d), lambda i, k: (_group_of((i + t0) * tm, n_ctx, lat_len), which, 0, 0))
    in_specs = [
        pl.BlockSpec((tm, tk), lambda i, k: (i + t0, k)),
        pl.BlockSpec((None, tk, d), lambda i, k: (layer, k, 0)),
        pl.BlockSpec((tm, d), lambda i, k: (i + t0, 0)),
        mod_spec(MOD_GATE2),
        pl.BlockSpec((1, d), lambda i, k: (0, 0)),
    ]
    args = [act, w_all, x, mod, g_next.reshape(1, d)]
    if final:
        out_specs = pl.BlockSpec((tm, d), lambda i, k: (i, 0))
        out_shape = jax.ShapeDtypeStruct((n_rows, d), F32)
    else:
        in_specs += [mod_spec(MOD_SHIFT1), mod_spec(MOD_SCALE1)]
        args += [mod_next, mod_next]
        out_specs = [pl.BlockSpec((tm, d), lambda i, k: (i, 0)), pl.BlockSpec((tm, d), lambda i, k: (i, 0))]
        out_shape = [jax.ShapeDtypeStruct((n_rows, d), F32), jax.ShapeDtypeStruct((n_rows, d), BF16)]
    kern = functools.partial(_ffn_down_kernel, final=final, d=d, tm=tm, n_k=n_k)
    return pl.pallas_call(
        kern,
        grid=(n_rows // tm, n_k),
        in_specs=in_specs,
        out_specs=out_specs,
        out_shape=out_shape,
        compiler_params=_params(("arbitrary", "arbitrary"), 60),
        name="ffn_down",
    )(*args)


HALO = BF16_SUBLANES


def _ffn_up_kernel(h_ref, hp_ref, hn_ref, wg_ref, wu_ref, cg_ref, cu_ref, bg_ref, bu_ref, o_ref, hb_ref,
                   *, tm, n_ctx, ctx_len, lat_len):
    i = pl.program_id(0)
    row0 = i * tm
    is_ctx = row0 < n_ctx
    assert ctx_len & (ctx_len - 1) == 0 and lat_len & (lat_len - 1) == 0 and n_ctx % lat_len == 0

    def seq_pos(r):
        return jnp.where(is_ctx, r & (ctx_len - 1), r & (lat_len - 1))

    def seq_len():
        return jnp.where(is_ctx, ctx_len, lat_len)

    @pl.when(pl.program_id(1) == 0)
    def _():
        hb_ref[HALO:HALO + tm, :] = h_ref[...]
        keep_prev = (seq_pos(row0) != 0).astype(F32)
        keep_next = (seq_pos(row0 + tm - 1) != seq_len() - 1).astype(F32)
        hb_ref[0:HALO, :] = (hp_ref[...].astype(F32) * keep_prev).astype(BF16)
        hb_ref[HALO + tm:, :] = (hn_ref[...].astype(F32) * keep_next).astype(BF16)

    rows = row0 + lax.broadcasted_iota(jnp.int32, (tm, 1), 0)
    pos = seq_pos(rows)
    keep_p = (pos != 0).astype(F32)
    keep_n = (pos != seq_len() - 1).astype(F32)
    tot = tm + 2 * HALO

    def conv(w_ref, c_ref, b_ref):
        z = _dot(hb_ref[...], w_ref[...])
        zp = pltpu.roll(z, 1, axis=0)[HALO:HALO + tm]
        zn = pltpu.roll(z, tot - 1, axis=0)[HALO:HALO + tm]
        c = c_ref[...]
        return (zp * keep_p) * c[0:1] + z[HALO:HALO + tm] * c[1:2] + (zn * keep_n) * c[2:3] + b_ref[...]

    gate = conv(wg_ref, cg_ref, bg_ref)
    up = conv(wu_ref, cu_ref, bu_ref)
    o_ref[...] = (gate * _sigmoid(gate) * up).astype(BF16)


def _ffn_up(h, w_all, conv_w_all, conv_b_all, layer, n_ctx, ctx_len, lat_len, tm=1024, tn=512):
    t, d = h.shape
    f = w_all.shape[2] // 2
    nj = f // tn
    hb = tm // HALO
    last_blk = t // HALO - 1
    kern = functools.partial(_ffn_up_kernel, tm=tm, n_ctx=n_ctx, ctx_len=ctx_len, lat_len=lat_len)
    return pl.pallas_call(
        kern,
        grid=(t // tm, nj),
        in_specs=[
            pl.BlockSpec((tm, d), lambda i, j: (i, 0)),
            pl.BlockSpec((HALO, d), lambda i, j: (jnp.maximum(i * hb - 1, 0), 0)),
            pl.BlockSpec((HALO, d), lambda i, j: (jnp.minimum((i + 1) * hb, last_blk), 0)),
            pl.BlockSpec((None, d, tn), lambda i, j: (layer, 0, j)),
            pl.BlockSpec((None, d, tn), lambda i, j: (layer, 0, j + nj)),
            pl.BlockSpec((None, 3, tn), lambda i, j: (layer, 0, j)),
            pl.BlockSpec((None, 3, tn), lambda i, j: (layer, 0, j + nj)),
            pl.BlockSpec((None, 1, tn), lambda i, j: (layer, 0, j)),
            pl.BlockSpec((None, 1, tn), lambda i, j: (layer, 0, j + nj)),
        ],
        out_specs=pl.BlockSpec((tm, tn), lambda i, j: (i, j)),
        out_shape=jax.ShapeDtypeStruct((t, f), BF16),
        scratch_shapes=[pltpu.VMEM((tm + 2 * HALO, d), BF16)],
        compiler_params=_params(("arbitrary", "arbitrary"), 48),
        name="ffn_up",
    )(h, h, h, w_all, w_all, conv_w_all, conv_w_all, conv_b_all, conv_b_all)


def _attend(score_blocks, value_blocks, sink=None):
    m = jnp.max(score_blocks[0], axis=-1, keepdims=True)
    for s in score_blocks[1:]:
        m = jnp.maximum(m, jnp.max(s, axis=-1, keepdims=True))
    if sink is not None:
        m = jnp.maximum(m, sink)
    den = jnp.exp(sink - m) if sink is not None else 0.0
    out = None
    for s, v in zip(score_blocks, value_blocks):
        p = jnp.exp(s - m)
        den = den + jnp.sum(p, axis=-1, keepdims=True)
        o = _dot(p.astype(BF16), v)
        out = o if out is None else out + o
    return out / den


def _scaled_q(q):
    return (q * ATT_SCALE).astype(BF16)


def _ctx_attn_kernel(*refs, n_heads, n_kv, has_sink):
    if has_sink:
        sink_ref, q_ref, k_ref, v_ref, o_ref = refs
    else:
        q_ref, k_ref, v_ref, o_ref = refs
    grp = n_heads // n_kv
    for kv in range(n_kv):
        ksl = slice(kv * HEAD_DIM, (kv + 1) * HEAD_DIM)
        k = k_ref[:, ksl].astype(BF16)
        v = v_ref[:, ksl].astype(BF16)
        for gi in range(grp):
            h = kv * grp + gi
            hsl = slice(h * HEAD_DIM, (h + 1) * HEAD_DIM)
            s = _dot_t(q_ref[:, hsl], k)
            sink = sink_ref[h] if has_sink else None
            o_ref[:, hsl] = _attend([s], [v], sink).astype(BF16)


def _ctx_attn(q, k, v, n_seq, seq_len, n_heads, n_kv, sink=None):
    qw = n_heads * HEAD_DIM
    kw = n_kv * HEAD_DIM
    has_sink = sink is not None
    in_specs = [
        pl.BlockSpec((seq_len, qw), lambda b: (b, 0)),
        pl.BlockSpec((seq_len, kw), lambda b: (b, 0)),
        pl.BlockSpec((seq_len, kw), lambda b: (b, 0)),
    ]
    args = [q, k, v]
    if has_sink:
        in_specs = [pl.BlockSpec(memory_space=pltpu.SMEM)] + in_specs
        args = [sink] + args
    return pl.pallas_call(
        functools.partial(_ctx_attn_kernel, n_heads=n_heads, n_kv=n_kv, has_sink=has_sink),
        grid=(n_seq,),
        in_specs=in_specs,
        out_specs=pl.BlockSpec((seq_len, qw), lambda b: (b, 0)),
        out_shape=jax.ShapeDtypeStruct((n_seq * seq_len, qw), BF16),
        compiler_params=_params(("arbitrary",), 32),
        name="ctx_attn",
    )(*args)


NA_QROWS = 8
NA_KROWS = 16
NA_QBLK = NA_QROWS * GRID_W
NA_KBLK = NA_KROWS * GRID_W
NA_HEADS_PER_STEP = 4


def _na_key_row0(blk, rows):
    return np.clip(blk * NA_QROWS - NA_WIN_R // 2, 0, rows - NA_KROWS)


def _na_row_offsets(rows):
    nblk = rows // NA_QROWS
    a_idx = np.full((3, NA_QROWS, NA_KROWS), -1, np.int32)
    for v, blk in enumerate((0, 1, nblk - 1)):
        kr0 = _na_key_row0(blk, rows)
        for ql in range(NA_QROWS):
            qr = blk * NA_QROWS + ql
            start = np.clip(qr - NA_WIN_R // 2, 0, rows - NA_WIN_R)
            for kl in range(NA_KROWS):
                kr = kr0 + kl
                if start <= kr < start + NA_WIN_R:
                    a_idx[v, ql, kl] = kr - qr + NA_WIN_R - 1
    return a_idx


def _na_bias_kernel(rpb_ref, o_ref, *, a_idx):
    h = pl.program_id(0)
    n_dr = 2 * NA_WIN_R - 1
    n_dc = 2 * NA_WIN_C - 1
    qc = lax.broadcasted_iota(jnp.int32, (GRID_W, GRID_W), 0)
    kc = lax.broadcasted_iota(jnp.int32, (GRID_W, GRID_W), 1)
    col_start = jnp.clip(qc - NA_WIN_C // 2, 0, GRID_W - NA_WIN_C)
    col_ok = (kc >= col_start) & (kc < col_start + NA_WIN_C)
    dc = jnp.clip(kc - qc + NA_WIN_C - 1, 0, n_dc - 1)
    is_dc = [dc == j for j in range(n_dc)]
    masked = jnp.full((GRID_W, GRID_W), NEG_INF, F32)
    tiles = []
    for a in range(n_dr):
        t = jnp.zeros((GRID_W, GRID_W), F32)
        for j in range(n_dc):
            t = jnp.where(is_dc[j], rpb_ref[(h * n_dr + a) * n_dc + j], t)
        tiles.append(jnp.where(col_ok, t, NEG_INF))
    for v in range(a_idx.shape[0]):
        for ql in range(NA_QROWS):
            for kl in range(NA_KROWS):
                a = int(a_idx[v, ql, kl])
                o_ref[v, 0, ql * GRID_W:(ql + 1) * GRID_W, kl * GRID_W:(kl + 1) * GRID_W] = (
                    tiles[a] if a >= 0 else masked)


def _na_bias_tables(rpb, rows):
    h = rpb.shape[0]
    a_idx = _na_row_offsets(rows)
    return pl.pallas_call(
        functools.partial(_na_bias_kernel, a_idx=a_idx),
        grid=(h,),
        in_specs=[pl.BlockSpec(memory_space=pltpu.SMEM)],
        out_specs=pl.BlockSpec((3, 1, NA_QBLK, NA_KBLK), lambda hh: (0, hh, 0, 0)),
        out_shape=jax.ShapeDtypeStruct((3, h, NA_QBLK, NA_KBLK), F32),
        compiler_params=_params(("arbitrary",), 32),
        name="na_bias",
    )(rpb.astype(F32).reshape(-1))


def _na_kernel(q_ref, k_ref, v_ref, kc_ref, vc_ref, bias_ref, o_ref, *, rows):
    i = pl.program_id(2)
    kr0 = jnp.clip(i * NA_QROWS - NA_WIN_R // 2, 0, rows - NA_KROWS)
    k0 = pl.multiple_of(kr0 * GRID_W, GRID_W * (NA_WIN_R // 2))
    for hh in range(NA_HEADS_PER_STEP):
        hsl = slice(hh * HEAD_DIM, (hh + 1) * HEAD_DIM)
        q = q_ref[:, hsl]
        kw = k_ref[pl.ds(k0, NA_KBLK), hsl]
        vw = v_ref[pl.ds(k0, NA_KBLK), hsl]
        s_loc = _dot_t(q, kw) + bias_ref[0, hh]
        s_ctx = _dot_t(q, kc_ref[0, :, hsl].astype(BF16))
        o_ref[:, hsl] = _attend([s_loc, s_ctx], [vw, vc_ref[0, :, hsl].astype(BF16)]).astype(BF16)


def _na_attn(q, k, v, cache_k, cache_v, bias, n_ctx, n_lat, lat_len):
    rows = lat_len // GRID_W
    nblk = lat_len // NA_QBLK
    assert n_ctx % NA_QBLK == 0
    qb0 = n_ctx // NA_QBLK
    hps = NA_HEADS_PER_STEP
    hw = hps * HEAD_DIM
    hg = NA_HEADS // hps
    c = cache_k.shape[1]
    variant = lambda i: jnp.where(i == 0, 0, jnp.where(i == nblk - 1, 2, 1))
    return pl.pallas_call(
        functools.partial(_na_kernel, rows=rows),
        grid=(n_lat, hg, nblk),
        in_specs=[
            pl.BlockSpec((NA_QBLK, hw), lambda b, hh, i: (qb0 + b * nblk + i, hh)),
            pl.BlockSpec((lat_len, hw), lambda b, hh, i: (b, hh)),
            pl.BlockSpec((lat_len, hw), lambda b, hh, i: (b, hh)),
            pl.BlockSpec((1, c, hw), lambda b, hh, i: (b, 0, hh)),
            pl.BlockSpec((1, c, hw), lambda b, hh, i: (b, 0, hh)),
            pl.BlockSpec((1, hps, NA_QBLK, NA_KBLK), lambda b, hh, i: (variant(i), hh, 0, 0)),
        ],
        out_specs=pl.BlockSpec((NA_QBLK, hw), lambda b, hh, i: (b * nblk + i, hh)),
        out_shape=jax.ShapeDtypeStruct((n_lat * lat_len, NA_HEADS * HEAD_DIM), BF16),
        compiler_params=_params(("arbitrary", "arbitrary", "arbitrary"), 56),
        name="na_attn",
    )(q, k, v, cache_k, cache_v, bias)


SWA_QBLK = 512
SWA_KBLK = SWA_QBLK + 2 * SWA_WINDOW


def _rope_tables(length):
    half = HEAD_DIM // 2
    freqs = ROPE_BASE ** (-jnp.arange(0, half, 2, dtype=F32) / half)
    t = jnp.arange(length)
    ang_r = (t // GRID_W).astype(F32)[:, None] * freqs[None, :]
    ang_c = (t % GRID_W).astype(F32)[:, None] * freqs[None, :]
    ang = jnp.concatenate([ang_r, ang_r, ang_c, ang_c], axis=1)
    sign = np.concatenate([-np.ones(half // 2), np.ones(half // 2)] * 2).astype(np.float32)
    return jnp.cos(ang), jnp.sin(ang) * jnp.asarray(sign)[None, :]


def _rope(x, cos, sin_signed):
    lane = lax.broadcasted_iota(jnp.int32, x.shape, 1)
    quarter = HEAD_DIM // 4
    lower = (lane & (2 * quarter - 1)) < quarter
    swapped = jnp.where(lower, pltpu.roll(x, HEAD_DIM - quarter, axis=1), pltpu.roll(x, quarter, axis=1))
    return x * cos + swapped * sin_signed


def _swa_kernel(sink_ref, q_ref, k_ref, v_ref, kc_ref, vc_ref, cos_ref, sin_ref, o_ref, *, lat_len):
    kv = pl.program_id(1)
    i = pl.program_id(2)
    grp = SWA_HEADS // SWA_KV_HEADS
    q0 = pl.multiple_of(i * SWA_QBLK, SWA_QBLK)
    k0 = pl.multiple_of(jnp.clip(i * SWA_QBLK - SWA_WINDOW, 0, lat_len - SWA_KBLK), SWA_WINDOW)
    kw = _rope(k_ref[pl.ds(k0, SWA_KBLK), :], cos_ref[pl.ds(k0, SWA_KBLK), :],
               sin_ref[pl.ds(k0, SWA_KBLK), :]).astype(BF16)
    vw = v_ref[pl.ds(k0, SWA_KBLK), :].astype(BF16)
    kc = kc_ref[0].astype(BF16)
    vc = vc_ref[0].astype(BF16)
    qpos = q0 + lax.broadcasted_iota(jnp.int32, (SWA_QBLK, SWA_KBLK), 0)
    kpos = k0 + lax.broadcasted_iota(jnp.int32, (SWA_QBLK, SWA_KBLK), 1)
    band = jnp.where(jnp.abs(qpos - kpos) <= SWA_WINDOW, 0.0, NEG_INF)
    cos_q = cos_ref[pl.ds(q0, SWA_QBLK), :]
    sin_q = sin_ref[pl.ds(q0, SWA_QBLK), :]
    for gi in range(grp):
        hsl = slice(gi * HEAD_DIM, (gi + 1) * HEAD_DIM)
        q = _scaled_q(_rope(q_ref[:, hsl], cos_q, sin_q))
        s_loc = _dot_t(q, kw) + band
        s_ctx = _dot_t(q, kc)
        sink = sink_ref[kv * grp + gi]
        o_ref[:, hsl] = _attend([s_loc, s_ctx], [vw, vc], sink).astype(BF16)


def _swa_attn(q, kv, cache_k, cache_v, sink, cos, sin_signed, n_lat, lat_len):
    grp = SWA_HEADS // SWA_KV_HEADS
    gw = grp * HEAD_DIM
    nblk = lat_len // SWA_QBLK
    c = cache_k.shape[1]
    return pl.pallas_call(
        functools.partial(_swa_kernel, lat_len=lat_len),
        grid=(n_lat, SWA_KV_HEADS, nblk),
        in_specs=[
            pl.BlockSpec(memory_space=pltpu.SMEM),
            pl.BlockSpec((SWA_QBLK, gw), lambda b, kv, i: (b * nblk + i, kv)),
            pl.BlockSpec((lat_len, HEAD_DIM), lambda b, kv, i: (b, kv)),
            pl.BlockSpec((lat_len, HEAD_DIM), lambda b, kv, i: (b, SWA_KV_HEADS + kv)),
            pl.BlockSpec((1, c, HEAD_DIM), lambda b, kv, i: (b, 0, kv)),
            pl.BlockSpec((1, c, HEAD_DIM), lambda b, kv, i: (b, 0, kv)),
            pl.BlockSpec((lat_len, HEAD_DIM), lambda b, kv, i: (0, 0)),
            pl.BlockSpec((lat_len, HEAD_DIM), lambda b, kv, i: (0, 0)),
        ],
        out_specs=pl.BlockSpec((SWA_QBLK, gw), lambda b, kv, i: (b * nblk + i, kv)),
        out_shape=jax.ShapeDtypeStruct((n_lat * lat_len, SWA_HEADS * HEAD_DIM), BF16),
        compiler_params=_params(("arbitrary", "arbitrary", "arbitrary"), 48),
        name="swa_attn",
    )(sink, q, kv, kv, cache_k, cache_v, cos, sin_signed)


def _s5_disc_kernel(are_ref, aim_ref, ldt_ref, bre_ref, bim_ref, abre_ref, abim_ref, bbre_ref, bbim_ref):
    a_re = are_ref[...]
    a_im = aim_ref[...]
    dt = jnp.exp(ldt_ref[...])
    mag = jnp.exp(a_re * dt)
    ang = a_im * dt
    ab_re = mag * jnp.cos(ang)
    ab_im = mag * jnp.sin(ang)
    den = a_re * a_re + a_im * a_im
    n_re = ab_re - 1.0
    f_re = (n_re * a_re + ab_im * a_im) / den
    f_im = (ab_im * a_re - n_re * a_im) / den
    b_re = bre_ref[...]
    b_im = bim_ref[...]
    abre_ref[...] = ab_re
    abim_ref[...] = ab_im
    bbre_ref[...] = f_re * b_re - f_im * b_im
    bbim_ref[...] = f_re * b_im + f_im * b_re


def _s5_discretise(a_re, a_im, log_dt, b_re, b_im):
    nd, g, p, c = b_re.shape
    shape2 = (nd * g, p * c)
    expand = lambda a: jnp.broadcast_to(a[..., None], (nd, g, p, c)).reshape(shape2)
    ldt = jnp.broadcast_to(log_dt[:, :, None, None], (nd, g, p, c)).reshape(shape2)
    out = pl.pallas_call(
        _s5_disc_kernel,
        out_shape=[jax.ShapeDtypeStruct(shape2, F32)] * 4,
        compiler_params=_params(None, 16),
        name="s5_discretise",
    )(expand(a_re), expand(a_im), ldt, b_re.reshape(shape2), b_im.reshape(shape2))
    ab_re, ab_im, bb_re, bb_im = [o.reshape(nd, g, p, c) for o in out]
    return ab_re[..., 0], ab_im[..., 0], bb_re, bb_im


def _block_diag(m, gb):
    g, a, b = m.shape
    m = m.reshape(g // gb, gb, a, b)
    eye = jnp.eye(gb, dtype=m.dtype)
    out = m[:, :, :, None, :] * eye[None, :, None, :, None]
    return out.reshape(g // gb, gb * a, gb * b)


def _cmul_add(ar, ai, xr, xi, br, bi):
    return ar * xr - ai * xi + br, ar * xi + ai * xr + bi


def _s5_kernel(*refs, latent):
    if latent:
        (u_ref, bbr_ref, bbi_ref, ccr_ref, cci_ref, ar_ref, ai_ref, d_ref, h0_ref,
         y_ref, xr_s, xi_s, cr_s, ci_s) = refs
    else:
        (u_ref, bbr_ref, bbi_ref, ccr_ref, cci_ref, ar_ref, ai_ref, d_ref,
         y_ref, fin_ref, xr_s, xi_s) = refs
    rows, steps = S5_ROWS, S5_STEPS
    u = u_ref[0]
    ub = u.astype(BF16)
    y = d_ref[0] * u
    for dirn in range(2):
        reverse = dirn == 1
        xr_s[...] = _dot(ub, bbr_ref[dirn, 0])
        xi_s[...] = _dot(ub, bbi_ref[dirn, 0])
        a_r1 = ar_ref[dirn, 0]
        a_i1 = ai_ref[dirn, 0]
        a_r = jnp.broadcast_to(a_r1, (rows, S5_GB_ST))
        a_i = jnp.broadcast_to(a_i1, (rows, S5_GB_ST))

        def time_rows(s):
            t = (steps - 1 - s) if reverse else s
            if isinstance(t, int):
                return pl.ds(t * rows, rows)
            return pl.ds(pl.multiple_of(t * rows, rows), rows)

        first = time_rows(0)
        final = time_rows(steps - 1)

        def scan_body(s, carry):
            pr, pi = carry
            sl = time_rows(s)
            nr, ni = _cmul_add(a_r, a_i, pr, pi, xr_s[sl, :], xi_s[sl, :])
            xr_s[sl, :] = nr
            xi_s[sl, :] = ni
            return nr, ni

        lax.fori_loop(1, steps, scan_body, (xr_s[first, :], xi_s[first, :]), unroll=4)

        if not latent:
            fin_ref[0, dirn, 0] = xr_s[final, :]
            fin_ref[0, dirn, 1] = xi_s[final, :]
        else:
            p_r, p_i = a_r1, a_i1
            for _ in range(int(math.log2(steps))):
                p_r, p_i = p_r * p_r - p_i * p_i, 2.0 * p_r * p_i
            assert 1 << int(math.log2(steps)) == steps
            e_r = xr_s[final, :]
            e_i = xi_s[final, :]
            c_r = h0_ref[0, 2 * dirn]
            c_i = h0_ref[0, 2 * dirn + 1]
            order = range(rows - 1, -1, -1) if reverse else range(rows)
            for j in order:
                cr_s[j:j + 1, :] = c_r
                ci_s[j:j + 1, :] = c_i
                c_r, c_i = _cmul_add(p_r, p_i, c_r, c_i, e_r[j:j + 1], e_i[j:j + 1])
            zero = jnp.zeros((rows, S5_GB_ST), F32)
            d_r, d_i = _cmul_add(a_r, a_i, cr_s[...], ci_s[...], zero, zero)

            def fix_body(s, carry):
                dr, di = carry
                sl = time_rows(s)
                xr_s[sl, :] = xr_s[sl, :] + dr
                xi_s[sl, :] = xi_s[sl, :] + di
                return _cmul_add(a_r, a_i, dr, di, zero, zero)

            lax.fori_loop(0, steps, fix_body, (d_r, d_i), unroll=4)

        y = y + _dot(xr_s[...].astype(BF16), ccr_ref[dirn, 0]) - _dot(xi_s[...].astype(BF16), cci_ref[dirn, 0])
    y_ref[0] = y


def _s5_scan(u_tm, tile0, tiles, bbr, bbi, ccr, cci, ab_re, ab_im, d, h0=None):
    _, tr, w = u_tm.shape
    assert tr == S5_STEPS * S5_ROWS
    nb = w // S5_GB_IN
    latent = h0 is not None
    in_specs = [
        pl.BlockSpec((1, tr, S5_GB_IN), lambda t, gb: (tile0 + t, 0, gb)),
        pl.BlockSpec((2, 1, S5_GB_IN, S5_GB_ST), lambda t, gb: (0, gb, 0, 0)),
        pl.BlockSpec((2, 1, S5_GB_IN, S5_GB_ST), lambda t, gb: (0, gb, 0, 0)),
        pl.BlockSpec((2, 1, S5_GB_ST, S5_GB_IN), lambda t, gb: (0, gb, 0, 0)),
        pl.BlockSpec((2, 1, S5_GB_ST, S5_GB_IN), lambda t, gb: (0, gb, 0, 0)),
        pl.BlockSpec((2, 1, 1, S5_GB_ST), lambda t, gb: (0, gb, 0, 0)),
        pl.BlockSpec((2, 1, 1, S5_GB_ST), lambda t, gb: (0, gb, 0, 0)),
        pl.BlockSpec((1, 1, S5_GB_IN), lambda t, gb: (gb, 0, 0)),
    ]
    args = [u_tm, bbr, bbi, ccr, cci, ab_re, ab_im, d]
    y_spec = pl.BlockSpec((1, tr, S5_GB_IN), lambda t, gb: (t, 0, gb))
    y_shape = jax.ShapeDtypeStruct((tiles, tr, w), F32)
    scratch = [pltpu.VMEM((tr, S5_GB_ST), F32), pltpu.VMEM((tr, S5_GB_ST), F32)]
    if latent:
        in_specs.append(pl.BlockSpec((1, 4, 1, S5_GB_ST), lambda t, gb: (t, 0, 0, gb)))
        args.append(h0)
        out_specs, out_shape = y_spec, y_shape
        scratch += [pltpu.VMEM((S5_ROWS, S5_GB_ST), F32), pltpu.VMEM((S5_ROWS, S5_GB_ST), F32)]
    else:
        out_specs = [y_spec, pl.BlockSpec((1, 2, 2, S5_ROWS, S5_GB_ST), lambda t, gb: (t, 0, 0, 0, gb))]
        out_shape = [y_shape, jax.ShapeDtypeStruct((tiles, 2, 2, S5_ROWS, nb * S5_GB_ST), F32)]
    return pl.pallas_call(
        functools.partial(_s5_kernel, latent=latent),
        grid=(tiles, nb),
        in_specs=in_specs,
        out_specs=out_specs,
        out_shape=out_shape,
        scratch_shapes=scratch,
        compiler_params=_params(("arbitrary", "arbitrary"), 48),
        name="s5_scan_lat" if latent else "s5_scan_ctx",
    )(*args)


def _glu_kernel(*refs, n_y, first_tiles):
    y_refs = refs[:n_y]
    w_ref, b_ref, o_ref = refs[n_y:]

    def run(y_ref):
        y = y_ref[...]
        g = 0.5 * y * (1.0 + jnp.tanh(math.sqrt(2.0 / math.pi) * (y + 0.044715 * (y * y * y))))
        z = _dot(g.astype(BF16), w_ref[...]) + b_ref[...]
        o_ref[...] = (g * _sigmoid(z)).astype(BF16)

    _select_rows(y_refs, pl.program_id(0), first_tiles, run)


def _glu(y, w, b, tm=1024):
    t, n = y.rows, y.width
    return pl.pallas_call(
        functools.partial(_glu_kernel, n_y=y.n_refs(), first_tiles=y.first_tiles(tm)),
        grid=(t // tm,),
        in_specs=y.specs(tm) + [
            pl.BlockSpec((n, n), lambda i: (0, 0)),
            pl.BlockSpec((1, n), lambda i: (0, 0)),
        ],
        out_specs=pl.BlockSpec((tm, n), lambda i: (i, 0)),
        out_shape=jax.ShapeDtypeStruct((t, n), BF16),
        compiler_params=_params(("arbitrary",), 32),
        name="s5_glu",
    )(*y.arrays, w, b.reshape(1, n))


def _s5_mixer(u, p, state, n_ctx_seq, ctx_len, n_lat, lat_len):
    t, w = u.shape
    g = w // S5_GROUP
    assert ctx_len == S5_STEPS and lat_len == S5_STEPS * S5_ROWS and n_ctx_seq % S5_ROWS == 0
    ab_re, ab_im, bb_re, bb_im = _s5_discretise(p["a_re"], p["a_im"], p["log_dt"], p["b_re"], p["b_im"])
    nb = g // S5_GB
    blocks = lambda m: jnp.stack([_block_diag(m[d].transpose(0, 2, 1), S5_GB) for d in range(2)]).astype(BF16)
    bbr, bbi = blocks(bb_re), blocks(bb_im)
    ccr, cci = blocks(p["c_re"].astype(F32)), blocks(p["c_im"].astype(F32))
    a_r = ab_re.reshape(2, nb, 1, S5_GB_ST)
    a_i = ab_im.reshape(2, nb, 1, S5_GB_ST)
    d = p["d"].astype(F32).reshape(nb, 1, S5_GB_IN)
    tiles = t // (S5_ROWS * S5_STEPS)
    ctx_tiles = n_ctx_seq // S5_ROWS
    tr = S5_STEPS * S5_ROWS
    u_tm = u.reshape(tiles, S5_ROWS, S5_STEPS, w).transpose(0, 2, 1, 3).reshape(tiles, tr, w)
    y_ctx, fin = _s5_scan(u_tm, 0, ctx_tiles, bbr, bbi, ccr, cci, a_r, a_i, d)
    h0 = state.astype(F32).transpose(0, 1, 4, 2, 3).reshape(n_lat, 4, 1, g * S5_STATE)
    y_lat = _s5_scan(u_tm, ctx_tiles, tiles - ctx_tiles, bbr, bbi, ccr, cci, a_r, a_i, d, h0=h0)
    y = _Rows(y_ctx.reshape(ctx_tiles * tr, w), y_lat.reshape((tiles - ctx_tiles) * tr, w))
    o = _glu(y, p["glu_w"].astype(BF16), p["glu_b"].astype(F32))
    o = o.reshape(tiles, S5_STEPS, S5_ROWS, w).transpose(0, 2, 1, 3).reshape(t, w)
    fin = fin.transpose(0, 3, 1, 4, 2).reshape(n_ctx_seq, 2, g, S5_STATE, 2)
    return o, fin


def kernel(x_prompt, x_sample, c, cache_na_k, cache_na_v, state_s5, cache_swa_k, cache_swa_v, c_ctx, norm_mix, norm_ffn, ada_w, ada_b, ab_w_in, ab_w_out, na_rpb, s5_a_re, s5_a_im, s5_log_dt, s5_b_re, s5_b_im, s5_c_re, s5_c_im, s5_d, s5_glu_w, s5_glu_b, swa_w_in, swa_w_out, swa_sink, ffn_w_up, ffn_conv_w, ffn_conv_b, ffn_w_down, final_norm):
    bp, lp, d = x_prompt.shape
    bs, ls, _ = x_sample.shape
    depth = ada_w.shape[0]
    n_ctx = bp * lp
    n_lat = bs * ls
    na_w = NA_HEADS * HEAD_DIM
    swa_q = SWA_HEADS * HEAD_DIM
    swa_kv = SWA_KV_HEADS * HEAD_DIM
    assert depth == 2, "layer pattern below is written for one A/B layer followed by one C layer"

    cond = jnp.concatenate([c_ctx[None, :], c, jnp.zeros((SUBLANES - 1 - bs, d), F32)], axis=0)
    mods = _adaln(cond, ada_w, ada_b).reshape(depth, SUBLANES, 6, 1, d)
    cos, sin_signed = _rope_tables(ls)
    w_up = _to_bf16(ffn_w_up)
    w_down = _to_bf16(ffn_w_down)
    conv_b = ffn_conv_b.reshape(depth, 1, -1)
    tn = 1024
    assert na_w == tn and swa_q == 2 * tn and 2 * swa_kv == tn

    x0 = _Rows(x_prompt.reshape(n_ctx, d), x_sample.reshape(n_lat, d))
    q, k_ctx, k_lat, v_ctx, v_lat, u = _proj(
        _norm_mod(x0, mods[0], norm_mix[0], n_ctx, ls), _to_bf16(ab_w_in[0]),
        [_ProjOut(0, 1, "all", BF16, scale=ATT_SCALE),
         _ProjOut(1, 2, "ctx", F32), _ProjOut(1, 2, "lat", BF16),
         _ProjOut(2, 3, "ctx", F32), _ProjOut(2, 3, "lat", BF16),
         _ProjOut(3, 4, "all", F32)], n_ctx, tn=tn)
    att_ctx = _ctx_attn(q, k_ctx, v_ctx, bp, lp, NA_HEADS, NA_HEADS)
    bias = _na_bias_tables(na_rpb[0], ls // GRID_W)
    att_lat = _na_attn(q, k_lat, v_lat, cache_na_k[:, 0].reshape(bs, -1, na_w),
                       cache_na_v[:, 0].reshape(bs, -1, na_w), bias, n_ctx, bs, ls)
    s5p = dict(a_re=s5_a_re[0], a_im=s5_a_im[0], log_dt=s5_log_dt[0], b_re=s5_b_re[0], b_im=s5_b_im[0],
               c_re=s5_c_re[0], c_im=s5_c_im[0], d=s5_d[0], glu_w=s5_glu_w[0], glu_b=s5_glu_b[0])
    s5o, s5_fin = _s5_mixer(u, s5p, state_s5[:, 0], bp, lp, bs, ls)
    x, h = _outproj([_Rows(att_ctx, att_lat), _Rows(s5o)], _to_bf16(ab_w_out[0]), x0, mods[0], norm_ffn[0],
                    n_ctx, ls)
    na_k = k_ctx.reshape(bp, 1, lp, NA_HEADS, HEAD_DIM)
    na_v = v_ctx.reshape(bp, 1, lp, NA_HEADS, HEAD_DIM)
    act = _ffn_up(h, w_up, ffn_conv_w, conv_b, 0, n_ctx, lp, ls)
    x, h = _ffn_down(act, w_down, 0, x, mods[0], mods[1], norm_mix[1], n_ctx, ls)

    q_ctx, q_lat, k_ctx, v_ctx, kv_lat = _proj(
        h, _to_bf16(swa_w_in[0]),
        [_ProjOut(0, 2, "ctx", BF16, scale=ATT_SCALE), _ProjOut(0, 2, "lat", F32),
         _ProjOut(2, 3, "ctx", F32, cols=(0, swa_kv)), _ProjOut(2, 3, "ctx", F32, cols=(swa_kv, 2 * swa_kv)),
         _ProjOut(2, 3, "lat", F32)], n_ctx, tn=tn)
    sink = swa_sink[0].astype(F32)
    att_ctx = _ctx_attn(q_ctx, k_ctx, v_ctx, bp, lp, SWA_HEADS, SWA_KV_HEADS, sink=sink)
    att_lat = _swa_attn(q_lat, kv_lat, cache_swa_k[:, 0].reshape(bs, -1, swa_kv),
                        cache_swa_v[:, 0].reshape(bs, -1, swa_kv), sink, cos, sin_signed, bs, ls)
    x, h = _outproj([_Rows(att_ctx, att_lat)], _to_bf16(swa_w_out[0]), _Rows(x), mods[1], norm_ffn[1], n_ctx, ls)
    swa_k = k_ctx.reshape(bp, 1, lp, SWA_KV_HEADS, HEAD_DIM)
    swa_v = v_ctx.reshape(bp, 1, lp, SWA_KV_HEADS, HEAD_DIM)
    act = _ffn_up(h, w_up, ffn_conv_w, conv_b, 1, n_ctx, lp, ls)
    y_p = _ffn_down(act, w_down, 1, x, mods[1], None, final_norm, n_ctx, ls, row0=0, n_rows=n_ctx)
    y_s = _ffn_down(act, w_down, 1, x, mods[1], None, final_norm, n_ctx, ls, row0=n_ctx, n_rows=n_lat)

    return (y_p.reshape(bp, lp, d), y_s.reshape(bs, ls, d), na_k, na_v, s5_fin[:, None], swa_k, swa_v)
```

```python
import functools
import math

import jax
import jax.numpy as jnp
import numpy as np
from jax import lax
from jax.experimental import pallas as pl
from jax.experimental.pallas import tpu as pltpu

F32 = jnp.float32
BF16 = jnp.bfloat16

GRID_W = 64
HEAD_DIM = 128
NA_HEADS = 8
NA_WIN_R = 8
NA_WIN_C = 16
S5_GROUP = 16
S5_STATE = 64
SWA_HEADS = 16
SWA_KV_HEADS = 4
SWA_WINDOW = 128
ROPE_BASE = 10000.0
EPS = 1e-6
NEG_INF = -1e30
ATT_SCALE = HEAD_DIM ** -0.5

VMEM_PHYSICAL_MIB = 64
SUBLANES = 8
BF16_SUBLANES = 16
LANES = 128

S5_STEPS = 256
S5_ROWS = 16
S5_GB = 8
S5_GB_IN = S5_GB * S5_GROUP
S5_GB_ST = S5_GB * S5_STATE


def _params(sem, vmem_mib):
    assert vmem_mib <= VMEM_PHYSICAL_MIB
    if sem is None:
        return pltpu.CompilerParams(vmem_limit_bytes=vmem_mib << 20)
    return pltpu.CompilerParams(dimension_semantics=sem, vmem_limit_bytes=vmem_mib << 20)


def _dot(a, b):
    return jnp.dot(a, b, preferred_element_type=F32)


def _dot_t(a, b):
    return lax.dot_general(a, b, (((1,), (1,)), ((), ())), preferred_element_type=F32)


def _sigmoid(x):
    return 1.0 / (1.0 + jnp.exp(-x))


class _Rows:
    def __init__(self, *arrays):
        assert len(arrays) in (1, 2)
        self.arrays = arrays
        self.width = arrays[0].shape[1]
        self.dtype = arrays[0].dtype
        self.rows = sum(a.shape[0] for a in arrays)

    def specs(self, tm, row_of_grid=lambda *g: g[0]):
        if len(self.arrays) == 1:
            return [pl.BlockSpec((tm, self.width), lambda *g: (row_of_grid(*g), 0))]
        na = self.arrays[0].shape[0] // tm
        return [
            pl.BlockSpec((tm, self.width), lambda *g: (jnp.minimum(row_of_grid(*g), na - 1), 0)),
            pl.BlockSpec((tm, self.width), lambda *g: (jnp.maximum(row_of_grid(*g) - na, 0), 0)),
        ]

    def n_refs(self):
        return len(self.arrays)

    def first_tiles(self, tm):
        return self.arrays[0].shape[0] // tm


def _select_rows(refs, tile, first_tiles, consume):
    if len(refs) == 1:
        consume(refs[0])
        return

    @pl.when(tile < first_tiles)
    def _():
        consume(refs[0])

    @pl.when(tile >= first_tiles)
    def _():
        consume(refs[1])


def _staged_rows(refs, tile, first_tiles, stage_ref):
    if len(refs) == 1:
        return refs[0]

    def copy(ref):
        stage_ref[...] = ref[...]
    _select_rows(refs, tile, first_tiles, copy)
    return stage_ref


def _adaln_kernel(c_ref, w_ref, b_ref, o_ref):
    c = c_ref[...]
    s = (c * _sigmoid(c)).astype(BF16)
    o_ref[0] = _dot(s, w_ref[0].astype(BF16)) + b_ref[0]


def _adaln(cond8, ada_w, ada_b):
    depth, d, n = ada_w.shape
    tn = 1024
    return pl.pallas_call(
        _adaln_kernel,
        grid=(depth, n // tn),
        in_specs=[
            pl.BlockSpec((SUBLANES, d), lambda l, j: (0, 0)),
            pl.BlockSpec((1, d, tn), lambda l, j: (l, 0, j)),
            pl.BlockSpec((1, 1, tn), lambda l, j: (l, 0, j)),
        ],
        out_specs=pl.BlockSpec((1, SUBLANES, tn), lambda l, j: (l, 0, j)),
        out_shape=jax.ShapeDtypeStruct((depth, SUBLANES, n), F32),
        compiler_params=_params(("arbitrary", "arbitrary"), 40),
        name="adaln",
    )(cond8, ada_w, ada_b.reshape(depth, 1, n))


MOD_SHIFT1, MOD_SCALE1, MOD_GATE1, MOD_SHIFT2, MOD_SCALE2, MOD_GATE2 = range(6)


def _group_of(row0, n_ctx, lat_len):
    return jnp.maximum((row0 - n_ctx) // lat_len + 1, 0)


def _mod_spec(which, tm, n_ctx, lat_len, d):
    return pl.BlockSpec((1, 1, 1, d), lambda *g: (_group_of(g[0] * tm, n_ctx, lat_len), which, 0, 0))


def _modulated(x, g, shift, scale):
    ms = jnp.mean(x * x, axis=-1, keepdims=True)
    y = x * lax.rsqrt(ms + EPS) * g
    return y * (1.0 + scale) + shift


def _norm_mod_kernel(*refs, n_x, first_tiles):
    x_refs = refs[:n_x]
    sh_ref, sc_ref, g_ref, o_ref = refs[n_x:]

    def run(x_ref):
        o_ref[...] = _modulated(x_ref[...], g_ref[...], sh_ref[0, 0], sc_ref[0, 0]).astype(BF16)

    _select_rows(x_refs, pl.program_id(0), first_tiles, run)


def _norm_mod(x, mod, g, n_ctx, lat_len, tm=512):
    t, d = x.rows, x.width
    return pl.pallas_call(
        functools.partial(_norm_mod_kernel, n_x=x.n_refs(), first_tiles=x.first_tiles(tm)),
        grid=(t // tm,),
        in_specs=x.specs(tm) + [
            _mod_spec(MOD_SHIFT1, tm, n_ctx, lat_len, d),
            _mod_spec(MOD_SCALE1, tm, n_ctx, lat_len, d),
            pl.BlockSpec((1, d), lambda i: (0, 0)),
        ],
        out_specs=pl.BlockSpec((tm, d), lambda i: (i, 0)),
        out_shape=jax.ShapeDtypeStruct((t, d), BF16),
        compiler_params=_params(("arbitrary",), 40),
        name="norm_mod",
    )(*x.arrays, mod, mod, g.reshape(1, d))


class _ProjOut:
    def __init__(self, j0, j1, part, dtype, scale=None, cols=None):
        assert part in ("all", "ctx", "lat") and (cols is None or j1 == j0 + 1)
        self.j0, self.j1, self.part, self.dtype, self.scale, self.cols = j0, j1, part, dtype, scale, cols


SIDE_LANES = 1024


class _SideCast:
    def __init__(self, w, steps):
        assert w.size % (steps * SIDE_LANES * BF16_SUBLANES) == 0
        self.shape = w.shape
        self.rows = w.size // (steps * SIDE_LANES)
        self.flat = w.reshape(steps * self.rows, SIDE_LANES)

    def spec(self, step_of_grid):
        return pl.BlockSpec((self.rows, SIDE_LANES), lambda *g: (step_of_grid(*g), 0))

    def out_shape(self):
        return jax.ShapeDtypeStruct(self.flat.shape, BF16)

    def restore(self, flat_bf16):
        return flat_bf16.reshape(self.shape)


def _run_side_casts(in_refs, out_refs):
    for s_ref, o_ref in zip(in_refs, out_refs):
        o_ref[...] = s_ref[...].astype(BF16)


def _proj_kernel(h_ref, w_ref, *rest, outs, ctx_tiles, n_sides):
    side_in = rest[:n_sides]
    o_refs = rest[n_sides:n_sides + len(outs)]
    side_out = rest[n_sides + len(outs):]
    _run_side_casts(side_in, side_out)
    i = pl.program_id(0)
    j = pl.program_id(1)
    for o, o_ref in zip(outs, o_refs):
        mine = (j >= o.j0) & (j < o.j1)
        if o.part == "ctx":
            mine = mine & (i < ctx_tiles)
        elif o.part == "lat":
            mine = mine & (i >= ctx_tiles)

        @pl.when(mine)
        def _(o=o, o_ref=o_ref):
            w = w_ref[...] if o.cols is None else w_ref[:, o.cols[0]:o.cols[1]]
            v = _dot(h_ref[...], w)
            if o.scale is not None:
                v = v * o.scale
            o_ref[...] = v.astype(o.dtype)


def _proj(h, w, outs, n_ctx, tm=1024, tn=1024, side_weights=()):
    t, d = h.shape
    n = w.shape[1]
    nc = n_ctx // tm
    nj = n // tn
    sides = [_SideCast(sw, (t // tm) * nj) for sw in side_weights]
    step = lambda i, j: i * nj + j
    n_rows = {"all": t, "ctx": n_ctx, "lat": t - n_ctx}

    def block_index(o, tiles, i, j):
        col = jnp.clip(j - o.j0, 0, tiles - 1)
        if o.part == "ctx":
            return jnp.minimum(i, nc - 1), jnp.where(i < nc, col, tiles - 1)
        if o.part == "lat":
            return jnp.maximum(i - nc, 0), jnp.where(i >= nc, col, 0)
        return i, col

    out_specs, out_shape = [], []
    for o in outs:
        width = tn if o.cols is None else o.cols[1] - o.cols[0]
        tiles = o.j1 - o.j0
        out_specs.append(pl.BlockSpec((tm, width), functools.partial(block_index, o, tiles)))
        out_shape.append(jax.ShapeDtypeStruct((n_rows[o.part], width * tiles), o.dtype))
    res = pl.pallas_call(
        functools.partial(_proj_kernel, outs=outs, ctx_tiles=nc, n_sides=len(sides)),
        grid=(t // tm, nj),
        in_specs=[pl.BlockSpec((tm, d), lambda i, j: (i, 0)), pl.BlockSpec((d, tn), lambda i, j: (0, j))]
        + [s.spec(step) for s in sides],
        out_specs=out_specs + [s.spec(step) for s in sides],
        out_shape=out_shape + [s.out_shape() for s in sides],
        compiler_params=_params(("arbitrary", "arbitrary"), 62),
        name="proj",
    )(h, w, *[s.flat for s in sides])
    n_out = len(outs)
    return list(res[:n_out]) + [s.restore(r) for s, r in zip(sides, res[n_out:])]


def _to_bf16_kernel(w_ref, o_ref):
    o_ref[...] = w_ref[...].astype(BF16)


def _to_bf16(w, block_bytes=8 << 20):
    shape = w.shape
    w2 = w.reshape(-1, shape[-1])
    r, c = w2.shape
    tr = max(BF16_SUBLANES, min(r, (block_bytes // (4 * c)) // BF16_SUBLANES * BF16_SUBLANES))
    while r % tr:
        tr -= BF16_SUBLANES
    out = pl.pallas_call(
        _to_bf16_kernel,
        grid=(r // tr,),
        in_specs=[pl.BlockSpec((tr, c), lambda i: (i, 0))],
        out_specs=pl.BlockSpec((tr, c), lambda i: (i, 0)),
        out_shape=jax.ShapeDtypeStruct((r, c), BF16),
        compiler_params=_params(("arbitrary",), 40),
        name="to_bf16",
    )(w2)
    return out.reshape(shape)


COL_CHUNK = 512


def _residual_norm(acc_of_cols, x_of_cols, gate_ref, xo_ref, d, tm):
    ss = jnp.zeros((tm, 1), F32)
    for c0 in range(0, d, COL_CHUNK):
        cols = slice(c0, c0 + COL_CHUNK)
        xn = x_of_cols(cols) + gate_ref[0, 0, :, cols] * acc_of_cols(cols)
        xo_ref[:, cols] = xn
        ss = ss + jnp.sum(xn * xn, axis=-1, keepdims=True)
    return lax.rsqrt(ss * (1.0 / d) + EPS)


def _write_modulated(xo_ref, inv, g_ref, sh_ref, sc_ref, h_ref, d):
    for c0 in range(0, d, COL_CHUNK):
        cols = slice(c0, c0 + COL_CHUNK)
        y = xo_ref[:, cols] * inv * g_ref[:, cols]
        h_ref[:, cols] = (y * (1.0 + sc_ref[0, 0, :, cols]) + sh_ref[0, 0, :, cols]).astype(BF16)


def _outproj_kernel(*refs, part_refs, part_first, n_x, x_first, d, tm):
    pos = 0
    lhs_refs = []
    for n in part_refs:
        lhs_refs.append(refs[pos:pos + n])
        pos += n
    x_refs = refs[pos:pos + n_x]
    pos += n_x
    w_ref, gate_ref, g_ref, sh_ref, sc_ref, xo_ref, h_ref = refs[pos:pos + 7]
    stage = list(refs[pos + 7:])
    tile = pl.program_id(0)

    lhs = []
    for prefs, first in zip(lhs_refs, part_first):
        lhs.append(_staged_rows(prefs, tile, first, stage.pop(0) if len(prefs) > 1 else None))
    x_ref = _staged_rows(x_refs, tile, x_first, stage.pop(0) if n_x > 1 else None)

    def acc_of_cols(cols):
        row = 0
        acc = None
        for l_ref in lhs:
            k = l_ref.shape[1]
            part = _dot(l_ref[...], w_ref[row:row + k, cols])
            acc = part if acc is None else acc + part
            row += k
        return acc

    inv = _residual_norm(acc_of_cols, lambda cols: x_ref[:, cols], gate_ref, xo_ref, d, tm)
    _write_modulated(xo_ref, inv, g_ref, sh_ref, sc_ref, h_ref, d)


def _outproj(parts, w, x, mod, g_next, n_ctx, lat_len, tm=512):
    t, d = x.rows, x.width
    in_specs, args, scratch = [], [], []
    for p in parts:
        in_specs += p.specs(tm)
        args += list(p.arrays)
    in_specs += x.specs(tm)
    args += list(x.arrays)
    in_specs += [
        pl.BlockSpec(w.shape, lambda i: (0, 0), pipeline_mode=pl.Buffered(1)),
        _mod_spec(MOD_GATE1, tm, n_ctx, lat_len, d),
        pl.BlockSpec((1, d), lambda i: (0, 0)),
        _mod_spec(MOD_SHIFT2, tm, n_ctx, lat_len, d),
        _mod_spec(MOD_SCALE2, tm, n_ctx, lat_len, d),
    ]
    args += [w, mod, g_next.reshape(1, d), mod, mod]
    scratch = [pltpu.VMEM((tm, p.width), p.dtype) for p in parts if p.n_refs() > 1]
    if x.n_refs() > 1:
        scratch.append(pltpu.VMEM((tm, d), F32))
    kern = functools.partial(
        _outproj_kernel, part_refs=[p.n_refs() for p in parts], part_first=[p.first_tiles(tm) for p in parts],
        n_x=x.n_refs(), x_first=x.first_tiles(tm), d=d, tm=tm)
    return pl.pallas_call(
        kern,
        grid=(t // tm,),
        in_specs=in_specs,
        out_specs=[pl.BlockSpec((tm, d), lambda i: (i, 0)), pl.BlockSpec((tm, d), lambda i: (i, 0))],
        out_shape=[jax.ShapeDtypeStruct((t, d), F32), jax.ShapeDtypeStruct((t, d), BF16)],
        scratch_shapes=scratch,
        compiler_params=_params(("arbitrary",), 56),
        name="outproj",
    )(*args)


def _ffn_down_kernel(act_ref, w_ref, x_ref, gate_ref, g_ref, *rest, final, d, tm, n_k):
    if final:
        (acc_ref,) = rest
    else:
        sh_ref, sc_ref, acc_ref, h_ref = rest
    k = pl.program_id(1)

    @pl.when(k == 0)
    def _():
        for c0 in range(0, d, COL_CHUNK):
            acc_ref[:, c0:c0 + COL_CHUNK] = _dot(act_ref[...], w_ref[:, c0:c0 + COL_CHUNK])

    @pl.when(k > 0)
    def _():
        for c0 in range(0, d, COL_CHUNK):
            acc_ref[:, c0:c0 + COL_CHUNK] += _dot(act_ref[...], w_ref[:, c0:c0 + COL_CHUNK])

    @pl.when(k == n_k - 1)
    def _():
        inv = _residual_norm(lambda cols: acc_ref[:, cols], lambda cols: x_ref[:, cols], gate_ref, acc_ref, d, tm)
        if final:
            for c0 in range(0, d, COL_CHUNK):
                cols = slice(c0, c0 + COL_CHUNK)
                acc_ref[:, cols] = acc_ref[:, cols] * inv * g_ref[:, cols]
        else:
            _write_modulated(acc_ref, inv, g_ref, sh_ref, sc_ref, h_ref, d)


def _ffn_down(act, w, x, mod, mod_next, g_next, n_ctx, lat_len, row0=0, n_rows=None, tm=1024):
    t, d = x.shape
    f = act.shape[1]
    final = mod_next is None
    n_rows = t if n_rows is None else n_rows
    tk = f // 4
    assert f % tk == 0 and tk % LANES == 0 and row0 % tm == 0 and n_rows % tm == 0
    n_k = f // tk
    t0 = row0 // tm
    mod_spec = lambda which: pl.BlockSpec(
        (1, 1, 1, d), lambda i, k: (_group_of((i + t0) * tm, n_ctx, lat_len), which, 0, 0))
    in_specs = [
        pl.BlockSpec((tm, tk), lambda i, k: (i + t0, k)),
        pl.BlockSpec((tk, d), lambda i, k: (k, 0)),
        pl.BlockSpec((tm, d), lambda i, k: (i + t0, 0)),
        mod_spec(MOD_GATE2),
        pl.BlockSpec((1, d), lambda i, k: (0, 0)),
    ]
    args = [act, w, x, mod, g_next.reshape(1, d)]
    if final:
        out_specs = pl.BlockSpec((tm, d), lambda i, k: (i, 0))
        out_shape = jax.ShapeDtypeStruct((n_rows, d), F32)
    else:
        in_specs += [mod_spec(MOD_SHIFT1), mod_spec(MOD_SCALE1)]
        args += [mod_next, mod_next]
        out_specs = [pl.BlockSpec((tm, d), lambda i, k: (i, 0)), pl.BlockSpec((tm, d), lambda i, k: (i, 0))]
        out_shape = [jax.ShapeDtypeStruct((n_rows, d), F32), jax.ShapeDtypeStruct((n_rows, d), BF16)]
    kern = functools.partial(_ffn_down_kernel, final=final, d=d, tm=tm, n_k=n_k)
    return pl.pallas_call(
        kern,
        grid=(n_rows // tm, n_k),
        in_specs=in_specs,
        out_specs=out_specs,
        out_shape=out_shape,
        compiler_params=_params(("arbitrary", "arbitrary"), 60),
        name="ffn_down",
    )(*args)


HALO = BF16_SUBLANES


def _ffn_up_kernel(h_ref, hp_ref, hn_ref, wg_ref, wu_ref, cg_ref, cu_ref, bg_ref, bu_ref, *rest,
                   tm, n_ctx, ctx_len, lat_len, n_sides):
    side_in = rest[:n_sides]
    o_ref = rest[n_sides]
    side_out = rest[n_sides + 1:2 * n_sides + 1]
    hb_ref = rest[2 * n_sides + 1]
    _run_side_casts(side_in, side_out)
    i = pl.program_id(0)
    row0 = i * tm
    is_ctx = row0 < n_ctx
    assert ctx_len & (ctx_len - 1) == 0 and lat_len & (lat_len - 1) == 0 and n_ctx % lat_len == 0

    def seq_pos(r):
        return jnp.where(is_ctx, r & (ctx_len - 1), r & (lat_len - 1))

    def seq_len():
        return jnp.where(is_ctx, ctx_len, lat_len)

    @pl.when(pl.program_id(1) == 0)
    def _():
        hb_ref[HALO:HALO + tm, :] = h_ref[...]
        keep_prev = (seq_pos(row0) != 0).astype(F32)
        keep_next = (seq_pos(row0 + tm - 1) != seq_len() - 1).astype(F32)
        hb_ref[0:HALO, :] = (hp_ref[...].astype(F32) * keep_prev).astype(BF16)
        hb_ref[HALO + tm:, :] = (hn_ref[...].astype(F32) * keep_next).astype(BF16)

    rows = row0 + lax.broadcasted_iota(jnp.int32, (tm, 1), 0)
    pos = seq_pos(rows)
    keep_p = (pos != 0).astype(F32)
    keep_n = (pos != seq_len() - 1).astype(F32)
    tot = tm + 2 * HALO

    def conv(w_ref, c_ref, b_ref):
        z = _dot(hb_ref[...], w_ref[...])
        zp = pltpu.roll(z, 1, axis=0)[HALO:HALO + tm]
        zn = pltpu.roll(z, tot - 1, axis=0)[HALO:HALO + tm]
        c = c_ref[...]
        return (zp * keep_p) * c[0:1] + z[HALO:HALO + tm] * c[1:2] + (zn * keep_n) * c[2:3] + b_ref[...]

    gate = conv(wg_ref, cg_ref, bg_ref)
    up = conv(wu_ref, cu_ref, bu_ref)
    o_ref[...] = (gate * _sigmoid(gate) * up).astype(BF16)


def _ffn_up(h, w, conv_w, conv_b, n_ctx, ctx_len, lat_len, tm=1024, tn=512, side_weights=()):
    t, d = h.shape
    f = w.shape[1] // 2
    nj = f // tn
    hb = tm // HALO
    last_blk = t // HALO - 1
    sides = [_SideCast(sw, (t // tm) * nj) for sw in side_weights]
    step = lambda i, j: i * nj + j
    conv_b = conv_b.reshape(1, 2 * f)
    kern = functools.partial(_ffn_up_kernel, tm=tm, n_ctx=n_ctx, ctx_len=ctx_len, lat_len=lat_len,
                             n_sides=len(sides))
    res = pl.pallas_call(
        kern,
        grid=(t // tm, nj),
        in_specs=[
            pl.BlockSpec((tm, d), lambda i, j: (i, 0)),
            pl.BlockSpec((HALO, d), lambda i, j: (jnp.maximum(i * hb - 1, 0), 0)),
            pl.BlockSpec((HALO, d), lambda i, j: (jnp.minimum((i + 1) * hb, last_blk), 0)),
            pl.BlockSpec((d, tn), lambda i, j: (0, j)),
            pl.BlockSpec((d, tn), lambda i, j: (0, j + nj)),
            pl.BlockSpec((3, tn), lambda i, j: (0, j)),
            pl.BlockSpec((3, tn), lambda i, j: (0, j + nj)),
            pl.BlockSpec((1, tn), lambda i, j: (0, j)),
            pl.BlockSpec((1, tn), lambda i, j: (0, j + nj)),
        ] + [s.spec(step) for s in sides],
        out_specs=[pl.BlockSpec((tm, tn), lambda i, j: (i, j))] + [s.spec(step) for s in sides],
        out_shape=[jax.ShapeDtypeStruct((t, f), BF16)] + [s.out_shape() for s in sides],
        scratch_shapes=[pltpu.VMEM((tm + 2 * HALO, d), BF16)],
        compiler_params=_params(("arbitrary", "arbitrary"), 48),
        name="ffn_up",
    )(h, h, h, w, w, conv_w, conv_w, conv_b, conv_b, *[s.flat for s in sides])
    return [res[0]] + [s.restore(r) for s, r in zip(sides, res[1:])]


def _attend(score_blocks, value_blocks, sink=None):
    m = jnp.max(score_blocks[0], axis=-1, keepdims=True)
    for s in score_blocks[1:]:
        m = jnp.maximum(m, jnp.max(s, axis=-1, keepdims=True))
    if sink is not None:
        m = jnp.maximum(m, sink)
    den = jnp.exp(sink - m) if sink is not None else 0.0
    out = None
    for s, v in zip(score_blocks, value_blocks):
        p = jnp.exp(s - m)
        den = den + jnp.sum(p, axis=-1, keepdims=True)
        o = _dot(p.astype(BF16), v)
        out = o if out is None else out + o
    return out / den


def _scaled_q(q):
    return (q * ATT_SCALE).astype(BF16)


CTX_SEQS_PER_STEP = 4


def _ctx_attn_kernel(*refs, n_heads, n_kv, has_sink, seq_len):
    if has_sink:
        sink_ref, q_ref, k_ref, v_ref, o_ref = refs
    else:
        q_ref, k_ref, v_ref, o_ref = refs
    grp = n_heads // n_kv

    def one_sequence(s, carry):
        rows = pl.ds(pl.multiple_of(s * seq_len, seq_len), seq_len)
        for kv in range(n_kv):
            ksl = slice(kv * HEAD_DIM, (kv + 1) * HEAD_DIM)
            k = k_ref[rows, ksl].astype(BF16)
            v = v_ref[rows, ksl].astype(BF16)
            for gi in range(grp):
                h = kv * grp + gi
                hsl = slice(h * HEAD_DIM, (h + 1) * HEAD_DIM)
                sc = _dot_t(q_ref[rows, hsl], k)
                sink = sink_ref[h] if has_sink else None
                o_ref[rows, hsl] = _attend([sc], [v], sink).astype(BF16)
        return carry

    lax.fori_loop(0, CTX_SEQS_PER_STEP, one_sequence, 0)


def _ctx_attn(q, k, v, n_seq, seq_len, n_heads, n_kv, sink=None):
    qw = n_heads * HEAD_DIM
    kw = n_kv * HEAD_DIM
    has_sink = sink is not None
    assert n_seq % CTX_SEQS_PER_STEP == 0
    blk = CTX_SEQS_PER_STEP * seq_len
    in_specs = [
        pl.BlockSpec((blk, qw), lambda b: (b, 0)),
        pl.BlockSpec((blk, kw), lambda b: (b, 0)),
        pl.BlockSpec((blk, kw), lambda b: (b, 0)),
    ]
    args = [q, k, v]
    if has_sink:
        in_specs = [pl.BlockSpec(memory_space=pltpu.SMEM)] + in_specs
        args = [sink] + args
    return pl.pallas_call(
        functools.partial(_ctx_attn_kernel, n_heads=n_heads, n_kv=n_kv, has_sink=has_sink, seq_len=seq_len),
        grid=(n_seq // CTX_SEQS_PER_STEP,),
        in_specs=in_specs,
        out_specs=pl.BlockSpec((blk, qw), lambda b: (b, 0)),
        out_shape=jax.ShapeDtypeStruct((n_seq * seq_len, qw), BF16),
        compiler_params=_params(("arbitrary",), 40),
        name="ctx_attn",
    )(*args)


NA_QROWS = 8
NA_KROWS = 16
NA_QBLK = NA_QROWS * GRID_W
NA_KBLK = NA_KROWS * GRID_W
NA_HEADS_PER_STEP = 4


def _na_key_row0(blk, rows):
    return np.clip(blk * NA_QROWS - NA_WIN_R // 2, 0, rows - NA_KROWS)


def _na_row_offsets(rows):
    nblk = rows // NA_QROWS
    a_idx = np.full((3, NA_QROWS, NA_KROWS), -1, np.int32)
    for v, blk in enumerate((0, 1, nblk - 1)):
        kr0 = _na_key_row0(blk, rows)
        for ql in range(NA_QROWS):
            qr = blk * NA_QROWS + ql
            start = np.clip(qr - NA_WIN_R // 2, 0, rows - NA_WIN_R)
            for kl in range(NA_KROWS):
                kr = kr0 + kl
                if start <= kr < start + NA_WIN_R:
                    a_idx[v, ql, kl] = kr - qr + NA_WIN_R - 1
    return a_idx


def _na_bias_kernel(rpb_ref, o_ref, *, a_idx):
    h = pl.program_id(0)
    n_dr = 2 * NA_WIN_R - 1
    n_dc = 2 * NA_WIN_C - 1
    qc = lax.broadcasted_iota(jnp.int32, (GRID_W, GRID_W), 0)
    kc = lax.broadcasted_iota(jnp.int32, (GRID_W, GRID_W), 1)
    col_start = jnp.clip(qc - NA_WIN_C // 2, 0, GRID_W - NA_WIN_C)
    col_ok = (kc >= col_start) & (kc < col_start + NA_WIN_C)
    dc = jnp.clip(kc - qc + NA_WIN_C - 1, 0, n_dc - 1)
    is_dc = [dc == j for j in range(n_dc)]
    masked = jnp.full((GRID_W, GRID_W), NEG_INF, F32)
    tiles = []
    for a in range(n_dr):
        t = jnp.zeros((GRID_W, GRID_W), F32)
        for j in range(n_dc):
            t = jnp.where(is_dc[j], rpb_ref[(h * n_dr + a) * n_dc + j], t)
        tiles.append(jnp.where(col_ok, t, NEG_INF))
    for v in range(a_idx.shape[0]):
        for ql in range(NA_QROWS):
            for kl in range(NA_KROWS):
                a = int(a_idx[v, ql, kl])
                o_ref[v, 0, ql * GRID_W:(ql + 1) * GRID_W, kl * GRID_W:(kl + 1) * GRID_W] = (
                    tiles[a] if a >= 0 else masked)


def _na_bias_tables(rpb, rows):
    h = rpb.shape[0]
    a_idx = _na_row_offsets(rows)
    return pl.pallas_call(
        functools.partial(_na_bias_kernel, a_idx=a_idx),
        grid=(h,),
        in_specs=[pl.BlockSpec(memory_space=pltpu.SMEM)],
        out_specs=pl.BlockSpec((3, 1, NA_QBLK, NA_KBLK), lambda hh: (0, hh, 0, 0)),
        out_shape=jax.ShapeDtypeStruct((3, h, NA_QBLK, NA_KBLK), F32),
        compiler_params=_params(("arbitrary",), 32),
        name="na_bias",
    )(rpb.astype(F32).reshape(-1))


def _na_kernel(q_ref, k_ref, v_ref, kc_ref, vc_ref, bias_ref, o_ref, *, rows):
    i = pl.program_id(2)
    kr0 = jnp.clip(i * NA_QROWS - NA_WIN_R // 2, 0, rows - NA_KROWS)
    k0 = pl.multiple_of(kr0 * GRID_W, GRID_W * (NA_WIN_R // 2))
    for hh in range(NA_HEADS_PER_STEP):
        hsl = slice(hh * HEAD_DIM, (hh + 1) * HEAD_DIM)
        q = q_ref[:, hsl]
        kw = k_ref[pl.ds(k0, NA_KBLK), hsl]
        vw = v_ref[pl.ds(k0, NA_KBLK), hsl]
        s_loc = _dot_t(q, kw) + bias_ref[0, hh]
        s_ctx = _dot_t(q, kc_ref[0, :, hsl].astype(BF16))
        o_ref[:, hsl] = _attend([s_loc, s_ctx], [vw, vc_ref[0, :, hsl].astype(BF16)]).astype(BF16)


def _na_attn(q, k, v, cache_k, cache_v, bias, n_ctx, n_lat, lat_len):
    rows = lat_len // GRID_W
    nblk = lat_len // NA_QBLK
    assert n_ctx % NA_QBLK == 0
    qb0 = n_ctx // NA_QBLK
    hps = NA_HEADS_PER_STEP
    hw = hps * HEAD_DIM
    hg = NA_HEADS // hps
    c = cache_k.shape[1]
    variant = lambda i: jnp.where(i == 0, 0, jnp.where(i == nblk - 1, 2, 1))
    return pl.pallas_call(
        functools.partial(_na_kernel, rows=rows),
        grid=(n_lat, hg, nblk),
        in_specs=[
            pl.BlockSpec((NA_QBLK, hw), lambda b, hh, i: (qb0 + b * nblk + i, hh)),
            pl.BlockSpec((lat_len, hw), lambda b, hh, i: (b, hh)),
            pl.BlockSpec((lat_len, hw), lambda b, hh, i: (b, hh)),
            pl.BlockSpec((1, c, hw), lambda b, hh, i: (b, 0, hh)),
            pl.BlockSpec((1, c, hw), lambda b, hh, i: (b, 0, hh)),
            pl.BlockSpec((1, hps, NA_QBLK, NA_KBLK), lambda b, hh, i: (variant(i), hh, 0, 0)),
        ],
        out_specs=pl.BlockSpec((NA_QBLK, hw), lambda b, hh, i: (b * nblk + i, hh)),
        out_shape=jax.ShapeDtypeStruct((n_lat * lat_len, NA_HEADS * HEAD_DIM), BF16),
        compiler_params=_params(("arbitrary", "arbitrary", "arbitrary"), 56),
        name="na_attn",
    )(q, k, v, cache_k, cache_v, bias)


SWA_QBLK = 512
SWA_KBLK = SWA_QBLK + 2 * SWA_WINDOW


def _rope_tables(length):
    half = HEAD_DIM // 2
    freqs = ROPE_BASE ** (-jnp.arange(0, half, 2, dtype=F32) / half)
    t = jnp.arange(length)
    ang_r = (t // GRID_W).astype(F32)[:, None] * freqs[None, :]
    ang_c = (t % GRID_W).astype(F32)[:, None] * freqs[None, :]
    ang = jnp.concatenate([ang_r, ang_r, ang_c, ang_c], axis=1)
    sign = np.concatenate([-np.ones(half // 2), np.ones(half // 2)] * 2).astype(np.float32)
    return jnp.cos(ang), jnp.sin(ang) * jnp.asarray(sign)[None, :]


def _rope(x, cos, sin_signed):
    lane = lax.broadcasted_iota(jnp.int32, x.shape, 1)
    quarter = HEAD_DIM // 4
    lower = (lane & (2 * quarter - 1)) < quarter
    swapped = jnp.where(lower, pltpu.roll(x, HEAD_DIM - quarter, axis=1), pltpu.roll(x, quarter, axis=1))
    return x * cos + swapped * sin_signed


def _swa_kernel(sink_ref, q_ref, k_ref, v_ref, kc_ref, vc_ref, cos_ref, sin_ref, o_ref, *, lat_len):
    kv = pl.program_id(1)
    i = pl.program_id(2)
    grp = SWA_HEADS // SWA_KV_HEADS
    q0 = pl.multiple_of(i * SWA_QBLK, SWA_QBLK)
    k0 = pl.multiple_of(jnp.clip(i * SWA_QBLK - SWA_WINDOW, 0, lat_len - SWA_KBLK), SWA_WINDOW)
    kw = _rope(k_ref[pl.ds(k0, SWA_KBLK), :], cos_ref[pl.ds(k0, SWA_KBLK), :],
               sin_ref[pl.ds(k0, SWA_KBLK), :]).astype(BF16)
    vw = v_ref[pl.ds(k0, SWA_KBLK), :].astype(BF16)
    kc = kc_ref[0].astype(BF16)
    vc = vc_ref[0].astype(BF16)
    qpos = q0 + lax.broadcasted_iota(jnp.int32, (SWA_QBLK, SWA_KBLK), 0)
    kpos = k0 + lax.broadcasted_iota(jnp.int32, (SWA_QBLK, SWA_KBLK), 1)
    band = jnp.where(jnp.abs(qpos - kpos) <= SWA_WINDOW, 0.0, NEG_INF)
    cos_q = cos_ref[pl.ds(q0, SWA_QBLK), :]
    sin_q = sin_ref[pl.ds(q0, SWA_QBLK), :]
    for gi in range(grp):
        hsl = slice(gi * HEAD_DIM, (gi + 1) * HEAD_DIM)
        q = _scaled_q(_rope(q_ref[:, hsl], cos_q, sin_q))
        s_loc = _dot_t(q, kw) + band
        s_ctx = _dot_t(q, kc)
        sink = sink_ref[kv * grp + gi]
        o_ref[:, hsl] = _attend([s_loc, s_ctx], [vw, vc], sink).astype(BF16)


def _swa_attn(q, kv, cache_k, cache_v, sink, cos, sin_signed, n_lat, lat_len):
    grp = SWA_HEADS // SWA_KV_HEADS
    gw = grp * HEAD_DIM
    nblk = lat_len // SWA_QBLK
    c = cache_k.shape[1]
    return pl.pallas_call(
        functools.partial(_swa_kernel, lat_len=lat_len),
        grid=(n_lat, SWA_KV_HEADS, nblk),
        in_specs=[
            pl.BlockSpec(memory_space=pltpu.SMEM),
            pl.BlockSpec((SWA_QBLK, gw), lambda b, kv, i: (b * nblk + i, kv)),
            pl.BlockSpec((lat_len, HEAD_DIM), lambda b, kv, i: (b, kv)),
            pl.BlockSpec((lat_len, HEAD_DIM), lambda b, kv, i: (b, SWA_KV_HEADS + kv)),
            pl.BlockSpec((1, c, HEAD_DIM), lambda b, kv, i: (b, 0, kv)),
            pl.BlockSpec((1, c, HEAD_DIM), lambda b, kv, i: (b, 0, kv)),
            pl.BlockSpec((lat_len, HEAD_DIM), lambda b, kv, i: (0, 0)),
            pl.BlockSpec((lat_len, HEAD_DIM), lambda b, kv, i: (0, 0)),
        ],
        out_specs=pl.BlockSpec((SWA_QBLK, gw), lambda b, kv, i: (b * nblk + i, kv)),
        out_shape=jax.ShapeDtypeStruct((n_lat * lat_len, SWA_HEADS * HEAD_DIM), BF16),
        compiler_params=_params(("arbitrary", "arbitrary", "arbitrary"), 48),
        name="swa_attn",
    )(sink, q, kv, kv, cache_k, cache_v, cos, sin_signed)


def _s5_disc_kernel(are_ref, aim_ref, ldt_ref, bre_ref, bim_ref, abre_ref, abim_ref, bbre_ref, bbim_ref):
    a_re = are_ref[...]
    a_im = aim_ref[...]
    dt = jnp.exp(ldt_ref[...])
    mag = jnp.exp(a_re * dt)
    ang = a_im * dt
    ab_re = mag * jnp.cos(ang)
    ab_im = mag * jnp.sin(ang)
    den = a_re * a_re + a_im * a_im
    n_re = ab_re - 1.0
    f_re = (n_re * a_re + ab_im * a_im) / den
    f_im = (ab_im * a_re - n_re * a_im) / den
    b_re = bre_ref[...]
    b_im = bim_ref[...]
    abre_ref[...] = ab_re
    abim_ref[...] = ab_im
    bbre_ref[...] = f_re * b_re - f_im * b_im
    bbim_ref[...] = f_re * b_im + f_im * b_re


def _s5_discretise(a_re, a_im, log_dt, b_re, b_im):
    nd, g, p, c = b_re.shape
    shape2 = (nd * g, p * c)
    expand = lambda a: jnp.broadcast_to(a[..., None], (nd, g, p, c)).reshape(shape2)
    ldt = jnp.broadcast_to(log_dt[:, :, None, None], (nd, g, p, c)).reshape(shape2)
    out = pl.pallas_call(
        _s5_disc_kernel,
        out_shape=[jax.ShapeDtypeStruct(shape2, F32)] * 4,
        compiler_params=_params(None, 16),
        name="s5_discretise",
    )(expand(a_re), expand(a_im), ldt, b_re.reshape(shape2), b_im.reshape(shape2))
    ab_re, ab_im, bb_re, bb_im = [o.reshape(nd, g, p, c) for o in out]
    return ab_re[..., 0], ab_im[..., 0], bb_re, bb_im


def _block_diag(m, gb):
    g, a, b = m.shape
    m = m.reshape(g // gb, gb, a, b)
    eye = jnp.eye(gb, dtype=m.dtype)
    out = m[:, :, :, None, :] * eye[None, :, None, :, None]
    return out.reshape(g // gb, gb * a, gb * b)


def _cmul_add(ar, ai, xr, xi, br, bi):
    return ar * xr - ai * xi + br, ar * xi + ai * xr + bi


def _s5_kernel(*refs, latent):
    if latent:
        (u_ref, bbr_ref, bbi_ref, ccr_ref, cci_ref, ar_ref, ai_ref, d_ref, h0_ref,
         y_ref, xr_s, xi_s, cr_s, ci_s) = refs
    else:
        (u_ref, bbr_ref, bbi_ref, ccr_ref, cci_ref, ar_ref, ai_ref, d_ref,
         y_ref, fin_ref, xr_s, xi_s) = refs
    rows, steps = S5_ROWS, S5_STEPS
    u = u_ref[0]
    ub = u.astype(BF16)
    y = d_ref[0] * u
    for dirn in range(2):
        reverse = dirn == 1
        xr_s[...] = _dot(ub, bbr_ref[dirn, 0])
        xi_s[...] = _dot(ub, bbi_ref[dirn, 0])
        a_r1 = ar_ref[dirn, 0]
        a_i1 = ai_ref[dirn, 0]
        a_r = jnp.broadcast_to(a_r1, (rows, S5_GB_ST))
        a_i = jnp.broadcast_to(a_i1, (rows, S5_GB_ST))

        def time_rows(s):
            t = (steps - 1 - s) if reverse else s
            if isinstance(t, int):
                return pl.ds(t * rows, rows)
            return pl.ds(pl.multiple_of(t * rows, rows), rows)

        first = time_rows(0)
        final = time_rows(steps - 1)

        def scan_body(s, carry):
            pr, pi = carry
            sl = time_rows(s)
            nr, ni = _cmul_add(a_r, a_i, pr, pi, xr_s[sl, :], xi_s[sl, :])
            xr_s[sl, :] = nr
            xi_s[sl, :] = ni
            return nr, ni

        lax.fori_loop(1, steps, scan_body, (xr_s[first, :], xi_s[first, :]), unroll=4)

        if not latent:
            fin_ref[0, dirn, 0] = xr_s[final, :]
            fin_ref[0, dirn, 1] = xi_s[final, :]
        else:
            p_r, p_i = a_r1, a_i1
            for _ in range(int(math.log2(steps))):
                p_r, p_i = p_r * p_r - p_i * p_i, 2.0 * p_r * p_i
            assert 1 << int(math.log2(steps)) == steps
            e_r = xr_s[final, :]
            e_i = xi_s[final, :]
            c_r = h0_ref[0, 2 * dirn]
            c_i = h0_ref[0, 2 * dirn + 1]
            order = range(rows - 1, -1, -1) if reverse else range(rows)
            for j in order:
                cr_s[j:j + 1, :] = c_r
                ci_s[j:j + 1, :] = c_i
                c_r, c_i = _cmul_add(p_r, p_i, c_r, c_i, e_r[j:j + 1], e_i[j:j + 1])
            zero = jnp.zeros((rows, S5_GB_ST), F32)
            d_r, d_i = _cmul_add(a_r, a_i, cr_s[...], ci_s[...], zero, zero)

            def fix_body(s, carry):
                dr, di = carry
                sl = time_rows(s)
                xr_s[sl, :] = xr_s[sl, :] + dr
                xi_s[sl, :] = xi_s[sl, :] + di
                return _cmul_add(a_r, a_i, dr, di, zero, zero)

            lax.fori_loop(0, steps, fix_body, (d_r, d_i), unroll=4)

        y = y + _dot(xr_s[...].astype(BF16), ccr_ref[dirn, 0]) - _dot(xi_s[...].astype(BF16), cci_ref[dirn, 0])
    y_ref[0] = y


def _s5_scan(u_tm, tile0, tiles, bbr, bbi, ccr, cci, ab_re, ab_im, d, h0=None):
    _, tr, w = u_tm.shape
    assert tr == S5_STEPS * S5_ROWS
    nb = w // S5_GB_IN
    latent = h0 is not None
    in_specs = [
        pl.BlockSpec((1, tr, S5_GB_IN), lambda t, gb: (tile0 + t, 0, gb)),
        pl.BlockSpec((2, 1, S5_GB_IN, S5_GB_ST), lambda t, gb: (0, gb, 0, 0)),
        pl.BlockSpec((2, 1, S5_GB_IN, S5_GB_ST), lambda t, gb: (0, gb, 0, 0)),
        pl.BlockSpec((2, 1, S5_GB_ST, S5_GB_IN), lambda t, gb: (0, gb, 0, 0)),
        pl.BlockSpec((2, 1, S5_GB_ST, S5_GB_IN), lambda t, gb: (0, gb, 0, 0)),
        pl.BlockSpec((2, 1, 1, S5_GB_ST), lambda t, gb: (0, gb, 0, 0)),
        pl.BlockSpec((2, 1, 1, S5_GB_ST), lambda t, gb: (0, gb, 0, 0)),
        pl.BlockSpec((1, 1, S5_GB_IN), lambda t, gb: (gb, 0, 0)),
    ]
    args = [u_tm, bbr, bbi, ccr, cci, ab_re, ab_im, d]
    y_spec = pl.BlockSpec((1, tr, S5_GB_IN), lambda t, gb: (t, 0, gb))
    y_shape = jax.ShapeDtypeStruct((tiles, tr, w), F32)
    scratch = [pltpu.VMEM((tr, S5_GB_ST), F32), pltpu.VMEM((tr, S5_GB_ST), F32)]
    if latent:
        in_specs.append(pl.BlockSpec((1, 4, 1, S5_GB_ST), lambda t, gb: (t, 0, 0, gb)))
        args.append(h0)
        out_specs, out_shape = y_spec, y_shape
        scratch += [pltpu.VMEM((S5_ROWS, S5_GB_ST), F32), pltpu.VMEM((S5_ROWS, S5_GB_ST), F32)]
    else:
        out_specs = [y_spec, pl.BlockSpec((1, 2, 2, S5_ROWS, S5_GB_ST), lambda t, gb: (t, 0, 0, 0, gb))]
        out_shape = [y_shape, jax.ShapeDtypeStruct((tiles, 2, 2, S5_ROWS, nb * S5_GB_ST), F32)]
    return pl.pallas_call(
        functools.partial(_s5_kernel, latent=latent),
        grid=(tiles, nb),
        in_specs=in_specs,
        out_specs=out_specs,
        out_shape=out_shape,
        scratch_shapes=scratch,
        compiler_params=_params(("arbitrary", "arbitrary"), 48),
        name="s5_scan_lat" if latent else "s5_scan_ctx",
    )(*args)


def _glu_kernel(*refs, n_y, first_tiles):
    y_refs = refs[:n_y]
    w_ref, b_ref, o_ref = refs[n_y:]

    def run(y_ref):
        y = y_ref[...]
        g = 0.5 * y * (1.0 + jnp.tanh(math.sqrt(2.0 / math.pi) * (y + 0.044715 * (y * y * y))))
        z = _dot(g.astype(BF16), w_ref[...]) + b_ref[...]
        o_ref[...] = (g * _sigmoid(z)).astype(BF16)

    _select_rows(y_refs, pl.program_id(0), first_tiles, run)


def _glu(y, w, b, tm=1024):
    t, n = y.rows, y.width
    return pl.pallas_call(
        functools.partial(_glu_kernel, n_y=y.n_refs(), first_tiles=y.first_tiles(tm)),
        grid=(t // tm,),
        in_specs=y.specs(tm) + [
            pl.BlockSpec((n, n), lambda i: (0, 0)),
            pl.BlockSpec((1, n), lambda i: (0, 0)),
        ],
        out_specs=pl.BlockSpec((tm, n), lambda i: (i, 0)),
        out_shape=jax.ShapeDtypeStruct((t, n), BF16),
        compiler_params=_params(("arbitrary",), 32),
        name="s5_glu",
    )(*y.arrays, w, b.reshape(1, n))


def _s5_mixer(u, p, state, n_ctx_seq, ctx_len, n_lat, lat_len):
    t, w = u.shape
    g = w // S5_GROUP
    assert ctx_len == S5_STEPS and lat_len == S5_STEPS * S5_ROWS and n_ctx_seq % S5_ROWS == 0
    ab_re, ab_im, bb_re, bb_im = _s5_discretise(p["a_re"], p["a_im"], p["log_dt"], p["b_re"], p["b_im"])
    nb = g // S5_GB
    blocks = lambda m: jnp.stack([_block_diag(m[d].transpose(0, 2, 1), S5_GB) for d in range(2)]).astype(BF16)
    bbr, bbi = blocks(bb_re), blocks(bb_im)
    ccr, cci = blocks(p["c_re"].astype(F32)), blocks(p["c_im"].astype(F32))
    a_r = ab_re.reshape(2, nb, 1, S5_GB_ST)
    a_i = ab_im.reshape(2, nb, 1, S5_GB_ST)
    d = p["d"].astype(F32).reshape(nb, 1, S5_GB_IN)
    tiles = t // (S5_ROWS * S5_STEPS)
    ctx_tiles = n_ctx_seq // S5_ROWS
    tr = S5_STEPS * S5_ROWS
    u_tm = u.reshape(tiles, S5_ROWS, S5_STEPS, w).transpose(0, 2, 1, 3).reshape(tiles, tr, w)
    y_ctx, fin = _s5_scan(u_tm, 0, ctx_tiles, bbr, bbi, ccr, cci, a_r, a_i, d)
    h0 = state.astype(F32).transpose(0, 1, 4, 2, 3).reshape(n_lat, 4, 1, g * S5_STATE)
    y_lat = _s5_scan(u_tm, ctx_tiles, tiles - ctx_tiles, bbr, bbi, ccr, cci, a_r, a_i, d, h0=h0)
    y = _Rows(y_ctx.reshape(ctx_tiles * tr, w), y_lat.reshape((tiles - ctx_tiles) * tr, w))
    o = _glu(y, p["glu_w"].astype(BF16), p["glu_b"].astype(F32))
    o = o.reshape(tiles, S5_STEPS, S5_ROWS, w).transpose(0, 2, 1, 3).reshape(t, w)
    fin = fin.transpose(0, 3, 1, 4, 2).reshape(n_ctx_seq, 2, g, S5_STATE, 2)
    return o, fin


def kernel(x_prompt, x_sample, c, cache_na_k, cache_na_v, state_s5, cache_swa_k, cache_swa_v, c_ctx, norm_mix, norm_ffn, ada_w, ada_b, ab_w_in, ab_w_out, na_rpb, s5_a_re, s5_a_im, s5_log_dt, s5_b_re, s5_b_im, s5_c_re, s5_c_im, s5_d, s5_glu_w, s5_glu_b, swa_w_in, swa_w_out, swa_sink, ffn_w_up, ffn_conv_w, ffn_conv_b, ffn_w_down, final_norm):
    bp, lp, d = x_prompt.shape
    bs, ls, _ = x_sample.shape
    depth = ada_w.shape[0]
    n_ctx = bp * lp
    n_lat = bs * ls
    na_w = NA_HEADS * HEAD_DIM
    swa_q = SWA_HEADS * HEAD_DIM
    swa_kv = SWA_KV_HEADS * HEAD_DIM
    assert depth == 2, "layer pattern below is written for one A/B layer followed by one C layer"

    cond = jnp.concatenate([c_ctx[None, :], c, jnp.zeros((SUBLANES - 1 - bs, d), F32)], axis=0)
    mods = _adaln(cond, ada_w, ada_b).reshape(depth, SUBLANES, 6, 1, d)
    cos, sin_signed = _rope_tables(ls)
    tn = 1024
    assert na_w == tn and swa_q == 2 * tn and 2 * swa_kv == tn

    x0 = _Rows(x_prompt.reshape(n_ctx, d), x_sample.reshape(n_lat, d))
    q, k_ctx, k_lat, v_ctx, v_lat, u, w_up0 = _proj(
        _norm_mod(x0, mods[0], norm_mix[0], n_ctx, ls), _to_bf16(ab_w_in[0]),
        [_ProjOut(0, 1, "all", BF16, scale=ATT_SCALE),
         _ProjOut(1, 2, "ctx", F32), _ProjOut(1, 2, "lat", BF16),
         _ProjOut(2, 3, "ctx", F32), _ProjOut(2, 3, "lat", BF16),
         _ProjOut(3, 4, "all", F32)], n_ctx, tn=tn, side_weights=[ffn_w_up[0]])
    att_ctx = _ctx_attn(q, k_ctx, v_ctx, bp, lp, NA_HEADS, NA_HEADS)
    bias = _na_bias_tables(na_rpb[0], ls // GRID_W)
    att_lat = _na_attn(q, k_lat, v_lat, cache_na_k[:, 0].reshape(bs, -1, na_w),
                       cache_na_v[:, 0].reshape(bs, -1, na_w), bias, n_ctx, bs, ls)
    s5p = dict(a_re=s5_a_re[0], a_im=s5_a_im[0], log_dt=s5_log_dt[0], b_re=s5_b_re[0], b_im=s5_b_im[0],
               c_re=s5_c_re[0], c_im=s5_c_im[0], d=s5_d[0], glu_w=s5_glu_w[0], glu_b=s5_glu_b[0])
    s5o, s5_fin = _s5_mixer(u, s5p, state_s5[:, 0], bp, lp, bs, ls)
    x, h = _outproj([_Rows(att_ctx, att_lat), _Rows(s5o)], _to_bf16(ab_w_out[0]), x0, mods[0], norm_ffn[0],
                    n_ctx, ls)
    na_k = k_ctx.reshape(bp, 1, lp, NA_HEADS, HEAD_DIM)
    na_v = v_ctx.reshape(bp, 1, lp, NA_HEADS, HEAD_DIM)
    act, w_up1, w_down0, w_down1 = _ffn_up(h, w_up0, ffn_conv_w[0], ffn_conv_b[0], n_ctx, lp, ls,
                                           side_weights=[ffn_w_up[1], ffn_w_down[0], ffn_w_down[1]])
    x, h = _ffn_down(act, w_down0, x, mods[0], mods[1], norm_mix[1], n_ctx, ls)

    q_ctx, q_lat, k_ctx, v_ctx, kv_lat = _proj(
        h, _to_bf16(swa_w_in[0]),
        [_ProjOut(0, 2, "ctx", BF16, scale=ATT_SCALE), _ProjOut(0, 2, "lat", F32),
         _ProjOut(2, 3, "ctx", F32, cols=(0, swa_kv)), _ProjOut(2, 3, "ctx", F32, cols=(swa_kv, 2 * swa_kv)),
         _ProjOut(2, 3, "lat", F32)], n_ctx, tn=tn)
    sink = swa_sink[0].astype(F32)
    att_ctx = _ctx_attn(q_ctx, k_ctx, v_ctx, bp, lp, SWA_HEADS, SWA_KV_HEADS, sink=sink)
    att_lat = _swa_attn(q_lat, kv_lat, cache_swa_k[:, 0].reshape(bs, -1, swa_kv),
                        cache_swa_v[:, 0].reshape(bs, -1, swa_kv), sink, cos, sin_signed, bs, ls)
    x, h = _outproj([_Rows(att_ctx, att_lat)], _to_bf16(swa_w_out[0]), _Rows(x), mods[1], norm_ffn[1], n_ctx, ls)
    swa_k = k_ctx.reshape(bp, 1, lp, SWA_KV_HEADS, HEAD_DIM)
    swa_v = v_ctx.reshape(bp, 1, lp, SWA_KV_HEADS, HEAD_DIM)
    (act,) = _ffn_up(h, w_up1, ffn_conv_w[1], ffn_conv_b[1], n_ctx, lp, ls)
    y_p = _ffn_down(act, w_down1, x, mods[1], None, final_norm, n_ctx, ls, row0=0, n_rows=n_ctx)
    y_s = _ffn_down(act, w_down1, x, mods[1], None, final_norm, n_ctx, ls, row0=n_ctx, n_rows=n_lat)

    return (y_p.reshape(bp, lp, d), y_s.reshape(bs, ls, d), na_k, na_v, s5_fin[:, None], swa_k, swa_v)
```

```python
import functools
import math

import jax
import jax.numpy as jnp
import numpy as np
from jax import lax
from jax.experimental import pallas as pl
from jax.experimental.pallas import tpu as pltpu

F32 = jnp.float32
BF16 = jnp.bfloat16

GRID_W = 64
HEAD_DIM = 128
NA_HEADS = 8
NA_WIN_R = 8
NA_WIN_C = 16
S5_GROUP = 16
S5_STATE = 64
SWA_HEADS = 16
SWA_KV_HEADS = 4
SWA_WINDOW = 128
ROPE_BASE = 10000.0
EPS = 1e-6
NEG_INF = -1e30
ATT_SCALE = HEAD_DIM ** -0.5

VMEM_PHYSICAL_MIB = 64
SUBLANES = 8
BF16_SUBLANES = 16
LANES = 128

S5_STEPS = 256
S5_ROWS = 16
S5_GB = 8
S5_GB_IN = S5_GB * S5_GROUP
S5_GB_ST = S5_GB * S5_STATE


def _params(sem, vmem_mib):
    assert vmem_mib <= VMEM_PHYSICAL_MIB
    if sem is None:
        return pltpu.CompilerParams(vmem_limit_bytes=vmem_mib << 20)
    return pltpu.CompilerParams(dimension_semantics=sem, vmem_limit_bytes=vmem_mib << 20)


def _dot(a, b):
    return jnp.dot(a, b, preferred_element_type=F32)


def _dot_t(a, b):
    return lax.dot_general(a, b, (((1,), (1,)), ((), ())), preferred_element_type=F32)


def _sigmoid(x):
    return 1.0 / (1.0 + jnp.exp(-x))


class _Rows:
    def __init__(self, *arrays):
        assert len(arrays) in (1, 2)
        self.arrays = arrays
        self.width = arrays[0].shape[1]
        self.dtype = arrays[0].dtype
        self.rows = sum(a.shape[0] for a in arrays)

    def specs(self, tm, row_of_grid=lambda *g: g[0]):
        if len(self.arrays) == 1:
            return [pl.BlockSpec((tm, self.width), lambda *g: (row_of_grid(*g), 0))]
        na = self.arrays[0].shape[0] // tm
        return [
            pl.BlockSpec((tm, self.width), lambda *g: (jnp.minimum(row_of_grid(*g), na - 1), 0)),
            pl.BlockSpec((tm, self.width), lambda *g: (jnp.maximum(row_of_grid(*g) - na, 0), 0)),
        ]

    def n_refs(self):
        return len(self.arrays)

    def first_tiles(self, tm):
        return self.arrays[0].shape[0] // tm


def _select_rows(refs, tile, first_tiles, consume):
    if len(refs) == 1:
        consume(refs[0])
        return

    @pl.when(tile < first_tiles)
    def _():
        consume(refs[0])

    @pl.when(tile >= first_tiles)
    def _():
        consume(refs[1])


def _staged_rows(refs, tile, first_tiles, stage_ref):
    if len(refs) == 1:
        return refs[0]

    def copy(ref):
        stage_ref[...] = ref[...]
    _select_rows(refs, tile, first_tiles, copy)
    return stage_ref


def _adaln_kernel(c_ref, w_ref, b_ref, o_ref):
    c = c_ref[...]
    s = (c * _sigmoid(c)).astype(BF16)
    o_ref[0] = _dot(s, w_ref[0].astype(BF16)) + b_ref[0]


def _adaln(cond8, ada_w, ada_b):
    depth, d, n = ada_w.shape
    tn = 1024
    return pl.pallas_call(
        _adaln_kernel,
        grid=(depth, n // tn),
        in_specs=[
            pl.BlockSpec((SUBLANES, d), lambda l, j: (0, 0)),
            pl.BlockSpec((1, d, tn), lambda l, j: (l, 0, j)),
            pl.BlockSpec((1, 1, tn), lambda l, j: (l, 0, j)),
        ],
        out_specs=pl.BlockSpec((1, SUBLANES, tn), lambda l, j: (l, 0, j)),
        out_shape=jax.ShapeDtypeStruct((depth, SUBLANES, n), F32),
        compiler_params=_params(("arbitrary", "arbitrary"), 40),
        name="adaln",
    )(cond8, ada_w, ada_b.reshape(depth, 1, n))


MOD_SHIFT1, MOD_SCALE1, MOD_GATE1, MOD_SHIFT2, MOD_SCALE2, MOD_GATE2 = range(6)


def _group_of(row0, n_ctx, lat_len):
    return jnp.maximum((row0 - n_ctx) // lat_len + 1, 0)


def _mod_spec(which, tm, n_ctx, lat_len, d):
    return pl.BlockSpec((1, 1, 1, d), lambda *g: (_group_of(g[0] * tm, n_ctx, lat_len), which, 0, 0))


def _modulated(x, g, shift, scale):
    ms = jnp.mean(x * x, axis=-1, keepdims=True)
    y = x * lax.rsqrt(ms + EPS) * g
    return y * (1.0 + scale) + shift


def _norm_mod_kernel(*refs, n_x, first_tiles):
    x_refs = refs[:n_x]
    sh_ref, sc_ref, g_ref, o_ref = refs[n_x:]

    def run(x_ref):
        o_ref[...] = _modulated(x_ref[...], g_ref[...], sh_ref[0, 0], sc_ref[0, 0]).astype(BF16)

    _select_rows(x_refs, pl.program_id(0), first_tiles, run)


def _norm_mod(x, mod, g, n_ctx, lat_len, tm=512):
    t, d = x.rows, x.width
    return pl.pallas_call(
        functools.partial(_norm_mod_kernel, n_x=x.n_refs(), first_tiles=x.first_tiles(tm)),
        grid=(t // tm,),
        in_specs=x.specs(tm) + [
            _mod_spec(MOD_SHIFT1, tm, n_ctx, lat_len, d),
            _mod_spec(MOD_SCALE1, tm, n_ctx, lat_len, d),
            pl.BlockSpec((1, d), lambda i: (0, 0)),
        ],
        out_specs=pl.BlockSpec((tm, d), lambda i: (i, 0)),
        out_shape=jax.ShapeDtypeStruct((t, d), BF16),
        compiler_params=_params(("arbitrary",), 40),
        name="norm_mod",
    )(*x.arrays, mod, mod, g.reshape(1, d))


class _ProjOut:
    def __init__(self, j0, j1, part, dtype, scale=None, cols=None):
        assert part in ("all", "ctx", "lat") and (cols is None or j1 == j0 + 1)
        self.j0, self.j1, self.part, self.dtype, self.scale, self.cols = j0, j1, part, dtype, scale, cols


class _SideCast:
    def __init__(self, w_stacked, layer, grid):
        gi, gj = grid
        _, r, c = w_stacked.shape
        self.w, self.layer, self.shape = w_stacked, layer, (r, c)
        if r % (gi * gj * BF16_SUBLANES) == 0:
            self.block = (r // (gi * gj), c)
            self.index = lambda i, j: (i * gj + j, 0)
        elif r % (gi * BF16_SUBLANES) == 0 and c % (gj * LANES) == 0:
            self.block = (r // gi, c // gj)
            self.index = lambda i, j: (i, j)
        else:
            assert r % (gj * BF16_SUBLANES) == 0 and c % (gi * LANES) == 0
            self.block = (r // gj, c // gi)
            self.index = lambda i, j: (j, i)

    def in_spec(self):
        return pl.BlockSpec((None,) + self.block, lambda i, j: (self.layer,) + self.index(i, j))

    def out_spec(self):
        return pl.BlockSpec(self.block, self.index)

    def out_shape(self):
        return jax.ShapeDtypeStruct(self.shape, BF16)


def _run_side_casts(in_refs, out_refs):
    for s_ref, o_ref in zip(in_refs, out_refs):
        o_ref[...] = s_ref[...].astype(BF16)


def _proj_kernel(h_ref, w_ref, *rest, outs, ctx_tiles, n_sides):
    side_in = rest[:n_sides]
    o_refs = rest[n_sides:n_sides + len(outs)]
    side_out = rest[n_sides + len(outs):]
    _run_side_casts(side_in, side_out)
    i = pl.program_id(0)
    j = pl.program_id(1)
    for o, o_ref in zip(outs, o_refs):
        mine = (j >= o.j0) & (j < o.j1)
        if o.part == "ctx":
            mine = mine & (i < ctx_tiles)
        elif o.part == "lat":
            mine = mine & (i >= ctx_tiles)

        @pl.when(mine)
        def _(o=o, o_ref=o_ref):
            w = w_ref[...] if o.cols is None else w_ref[:, o.cols[0]:o.cols[1]]
            v = _dot(h_ref[...], w)
            if o.scale is not None:
                v = v * o.scale
            o_ref[...] = v.astype(o.dtype)


def _proj(h, w, outs, n_ctx, tm=1024, tn=1024, side_weights=()):
    t, d = h.shape
    n = w.shape[1]
    nc = n_ctx // tm
    nj = n // tn
    sides = [_SideCast(sw, layer, (t // tm, nj)) for sw, layer in side_weights]
    n_rows = {"all": t, "ctx": n_ctx, "lat": t - n_ctx}

    def block_index(o, tiles, i, j):
        col = jnp.clip(j - o.j0, 0, tiles - 1)
        if o.part == "ctx":
            return jnp.minimum(i, nc - 1), jnp.where(i < nc, col, tiles - 1)
        if o.part == "lat":
            return jnp.maximum(i - nc, 0), jnp.where(i >= nc, col, 0)
        return i, col

    out_specs, out_shape = [], []
    for o in outs:
        width = tn if o.cols is None else o.cols[1] - o.cols[0]
        tiles = o.j1 - o.j0
        out_specs.append(pl.BlockSpec((tm, width), functools.partial(block_index, o, tiles)))
        out_shape.append(jax.ShapeDtypeStruct((n_rows[o.part], width * tiles), o.dtype))
    res = pl.pallas_call(
        functools.partial(_proj_kernel, outs=outs, ctx_tiles=nc, n_sides=len(sides)),
        grid=(t // tm, nj),
        in_specs=[pl.BlockSpec((tm, d), lambda i, j: (i, 0)), pl.BlockSpec((d, tn), lambda i, j: (0, j))]
        + [s.in_spec() for s in sides],
        out_specs=out_specs + [s.out_spec() for s in sides],
        out_shape=out_shape + [s.out_shape() for s in sides],
        compiler_params=_params(("arbitrary", "arbitrary"), 62),
        name="proj",
    )(h, w, *[s.w for s in sides])
    return list(res)


def _to_bf16_kernel(w_ref, o_ref):
    o_ref[...] = w_ref[...].astype(BF16)


def _to_bf16(w, block_bytes=8 << 20):
    shape = w.shape
    w2 = w.reshape(-1, shape[-1])
    r, c = w2.shape
    tr = max(BF16_SUBLANES, min(r, (block_bytes // (4 * c)) // BF16_SUBLANES * BF16_SUBLANES))
    while r % tr:
        tr -= BF16_SUBLANES
    out = pl.pallas_call(
        _to_bf16_kernel,
        grid=(r // tr,),
        in_specs=[pl.BlockSpec((tr, c), lambda i: (i, 0))],
        out_specs=pl.BlockSpec((tr, c), lambda i: (i, 0)),
        out_shape=jax.ShapeDtypeStruct((r, c), BF16),
        compiler_params=_params(("arbitrary",), 40),
        name="to_bf16",
    )(w2)
    return out.reshape(shape)


COL_CHUNK = 512


def _residual_norm(acc_of_cols, x_of_cols, gate_ref, xo_ref, d, tm):
    ss = jnp.zeros((tm, 1), F32)
    for c0 in range(0, d, COL_CHUNK):
        cols = slice(c0, c0 + COL_CHUNK)
        xn = x_of_cols(cols) + gate_ref[0, 0, :, cols] * acc_of_cols(cols)
        xo_ref[:, cols] = xn
        ss = ss + jnp.sum(xn * xn, axis=-1, keepdims=True)
    return lax.rsqrt(ss * (1.0 / d) + EPS)


def _write_modulated(xo_ref, inv, g_ref, sh_ref, sc_ref, h_ref, d):
    for c0 in range(0, d, COL_CHUNK):
        cols = slice(c0, c0 + COL_CHUNK)
        y = xo_ref[:, cols] * inv * g_ref[:, cols]
        h_ref[:, cols] = (y * (1.0 + sc_ref[0, 0, :, cols]) + sh_ref[0, 0, :, cols]).astype(BF16)


def _outproj_kernel(*refs, part_refs, part_first, n_x, x_first, d, tm):
    pos = 0
    lhs_refs = []
    for n in part_refs:
        lhs_refs.append(refs[pos:pos + n])
        pos += n
    x_refs = refs[pos:pos + n_x]
    pos += n_x
    w_ref, gate_ref, g_ref, sh_ref, sc_ref, xo_ref, h_ref = refs[pos:pos + 7]
    stage = list(refs[pos + 7:])
    tile = pl.program_id(0)

    lhs = []
    for prefs, first in zip(lhs_refs, part_first):
        lhs.append(_staged_rows(prefs, tile, first, stage.pop(0) if len(prefs) > 1 else None))
    x_ref = _staged_rows(x_refs, tile, x_first, stage.pop(0) if n_x > 1 else None)

    def acc_of_cols(cols):
        row = 0
        acc = None
        for l_ref in lhs:
            k = l_ref.shape[1]
            part = _dot(l_ref[...], w_ref[row:row + k, cols])
            acc = part if acc is None else acc + part
            row += k
        return acc

    inv = _residual_norm(acc_of_cols, lambda cols: x_ref[:, cols], gate_ref, xo_ref, d, tm)
    _write_modulated(xo_ref, inv, g_ref, sh_ref, sc_ref, h_ref, d)


def _outproj(parts, w, x, mod, g_next, n_ctx, lat_len, tm=512):
    t, d = x.rows, x.width
    in_specs, args, scratch = [], [], []
    for p in parts:
        in_specs += p.specs(tm)
        args += list(p.arrays)
    in_specs += x.specs(tm)
    args += list(x.arrays)
    in_specs += [
        pl.BlockSpec(w.shape, lambda i: (0, 0), pipeline_mode=pl.Buffered(1)),
        _mod_spec(MOD_GATE1, tm, n_ctx, lat_len, d),
        pl.BlockSpec((1, d), lambda i: (0, 0)),
        _mod_spec(MOD_SHIFT2, tm, n_ctx, lat_len, d),
        _mod_spec(MOD_SCALE2, tm, n_ctx, lat_len, d),
    ]
    args += [w, mod, g_next.reshape(1, d), mod, mod]
    scratch = [pltpu.VMEM((tm, p.width), p.dtype) for p in parts if p.n_refs() > 1]
    if x.n_refs() > 1:
        scratch.append(pltpu.VMEM((tm, d), F32))
    kern = functools.partial(
        _outproj_kernel, part_refs=[p.n_refs() for p in parts], part_first=[p.first_tiles(tm) for p in parts],
        n_x=x.n_refs(), x_first=x.first_tiles(tm), d=d, tm=tm)
    return pl.pallas_call(
        kern,
        grid=(t // tm,),
        in_specs=in_specs,
        out_specs=[pl.BlockSpec((tm, d), lambda i: (i, 0)), pl.BlockSpec((tm, d), lambda i: (i, 0))],
        out_shape=[jax.ShapeDtypeStruct((t, d), F32), jax.ShapeDtypeStruct((t, d), BF16)],
        scratch_shapes=scratch,
        compiler_params=_params(("arbitrary",), 56),
        name="outproj",
    )(*args)


def _ffn_down_kernel(act_ref, w_ref, x_ref, gate_ref, g_ref, *rest, final, d, tm, n_k):
    if final:
        (acc_ref,) = rest
    else:
        sh_ref, sc_ref, acc_ref, h_ref = rest
    k = pl.program_id(1)

    @pl.when(k == 0)
    def _():
        for c0 in range(0, d, COL_CHUNK):
            acc_ref[:, c0:c0 + COL_CHUNK] = _dot(act_ref[...], w_ref[:, c0:c0 + COL_CHUNK])

    @pl.when(k > 0)
    def _():
        for c0 in range(0, d, COL_CHUNK):
            acc_ref[:, c0:c0 + COL_CHUNK] += _dot(act_ref[...], w_ref[:, c0:c0 + COL_CHUNK])

    @pl.when(k == n_k - 1)
    def _():
        inv = _residual_norm(lambda cols: acc_ref[:, cols], lambda cols: x_ref[:, cols], gate_ref, acc_ref, d, tm)
        if final:
            for c0 in range(0, d, COL_CHUNK):
                cols = slice(c0, c0 + COL_CHUNK)
                acc_ref[:, cols] = acc_ref[:, cols] * inv * g_ref[:, cols]
        else:
            _write_modulated(acc_ref, inv, g_ref, sh_ref, sc_ref, h_ref, d)


def _ffn_down(act, w, x, mod, mod_next, g_next, n_ctx, lat_len, row0=0, n_rows=None, tm=1024):
    t, d = x.shape
    f = act.shape[1]
    final = mod_next is None
    n_rows = t if n_rows is None else n_rows
    tk = f // 4
    assert f % tk == 0 and tk % LANES == 0 and row0 % tm == 0 and n_rows % tm == 0
    n_k = f // tk
    t0 = row0 // tm
    mod_spec = lambda which: pl.BlockSpec(
        (1, 1, 1, d), lambda i, k: (_group_of((i + t0) * tm, n_ctx, lat_len), which, 0, 0))
    in_specs = [
        pl.BlockSpec((tm, tk), lambda i, k: (i + t0, k)),
        pl.BlockSpec((tk, d), lambda i, k: (k, 0)),
        pl.BlockSpec((tm, d), lambda i, k: (i + t0, 0)),
        mod_spec(MOD_GATE2),
        pl.BlockSpec((1, d), lambda i, k: (0, 0)),
    ]
    args = [act, w, x, mod, g_next.reshape(1, d)]
    if final:
        out_specs = pl.BlockSpec((tm, d), lambda i, k: (i, 0))
        out_shape = jax.ShapeDtypeStruct((n_rows, d), F32)
    else:
        in_specs += [mod_spec(MOD_SHIFT1), mod_spec(MOD_SCALE1)]
        args += [mod_next, mod_next]
        out_specs = [pl.BlockSpec((tm, d), lambda i, k: (i, 0)), pl.BlockSpec((tm, d), lambda i, k: (i, 0))]
        out_shape = [jax.ShapeDtypeStruct((n_rows, d), F32), jax.ShapeDtypeStruct((n_rows, d), BF16)]
    kern = functools.partial(_ffn_down_kernel, final=final, d=d, tm=tm, n_k=n_k)
    return pl.pallas_call(
        kern,
        grid=(n_rows // tm, n_k),
        in_specs=in_specs,
        out_specs=out_specs,
        out_shape=out_shape,
        compiler_params=_params(("arbitrary", "arbitrary"), 60),
        name="ffn_down",
    )(*args)


HALO = BF16_SUBLANES


def _ffn_up_kernel(h_ref, hp_ref, hn_ref, wg_ref, wu_ref, cg_ref, cu_ref, bg_ref, bu_ref, *rest,
                   tm, n_ctx, ctx_len, lat_len, n_sides):
    side_in = rest[:n_sides]
    o_ref = rest[n_sides]
    side_out = rest[n_sides + 1:2 * n_sides + 1]
    hb_ref = rest[2 * n_sides + 1]
    _run_side_casts(side_in, side_out)
    i = pl.program_id(0)
    row0 = i * tm
    is_ctx = row0 < n_ctx
    assert ctx_len & (ctx_len - 1) == 0 and lat_len & (lat_len - 1) == 0 and n_ctx % lat_len == 0

    def seq_pos(r):
        return jnp.where(is_ctx, r & (ctx_len - 1), r & (lat_len - 1))

    def seq_len():
        return jnp.where(is_ctx, ctx_len, lat_len)

    @pl.when(pl.program_id(1) == 0)
    def _():
        hb_ref[HALO:HALO + tm, :] = h_ref[...]
        keep_prev = (seq_pos(row0) != 0).astype(F32)
        keep_next = (seq_pos(row0 + tm - 1) != seq_len() - 1).astype(F32)
        hb_ref[0:HALO, :] = (hp_ref[...].astype(F32) * keep_prev).astype(BF16)
        hb_ref[HALO + tm:, :] = (hn_ref[...].astype(F32) * keep_next).astype(BF16)

    rows = row0 + lax.broadcasted_iota(jnp.int32, (tm, 1), 0)
    pos = seq_pos(rows)
    keep_p = (pos != 0).astype(F32)
    keep_n = (pos != seq_len() - 1).astype(F32)
    tot = tm + 2 * HALO

    def conv(w_ref, c_ref, b_ref):
        z = _dot(hb_ref[...], w_ref[...])
        zp = pltpu.roll(z, 1, axis=0)[HALO:HALO + tm]
        zn = pltpu.roll(z, tot - 1, axis=0)[HALO:HALO + tm]
        c = c_ref[...]
        return (zp * keep_p) * c[0:1] + z[HALO:HALO + tm] * c[1:2] + (zn * keep_n) * c[2:3] + b_ref[...]

    gate = conv(wg_ref, cg_ref, bg_ref)
    up = conv(wu_ref, cu_ref, bu_ref)
    o_ref[...] = (gate * _sigmoid(gate) * up).astype(BF16)


def _ffn_up(h, w, conv_w, conv_b, n_ctx, ctx_len, lat_len, tm=1024, tn=512, side_weights=()):
    t, d = h.shape
    f = w.shape[1] // 2
    nj = f // tn
    hb = tm // HALO
    last_blk = t // HALO - 1
    sides = [_SideCast(sw, layer, (t // tm, nj)) for sw, layer in side_weights]
    conv_b = conv_b.reshape(1, 2 * f)
    kern = functools.partial(_ffn_up_kernel, tm=tm, n_ctx=n_ctx, ctx_len=ctx_len, lat_len=lat_len,
                             n_sides=len(sides))
    res = pl.pallas_call(
        kern,
        grid=(t // tm, nj),
        in_specs=[
            pl.BlockSpec((tm, d), lambda i, j: (i, 0)),
            pl.BlockSpec((HALO, d), lambda i, j: (jnp.maximum(i * hb - 1, 0), 0)),
            pl.BlockSpec((HALO, d), lambda i, j: (jnp.minimum((i + 1) * hb, last_blk), 0)),
            pl.BlockSpec((d, tn), lambda i, j: (0, j)),
            pl.BlockSpec((d, tn), lambda i, j: (0, j + nj)),
            pl.BlockSpec((3, tn), lambda i, j: (0, j)),
            pl.BlockSpec((3, tn), lambda i, j: (0, j + nj)),
            pl.BlockSpec((1, tn), lambda i, j: (0, j)),
            pl.BlockSpec((1, tn), lambda i, j: (0, j + nj)),
        ] + [s.in_spec() for s in sides],
        out_specs=[pl.BlockSpec((tm, tn), lambda i, j: (i, j))] + [s.out_spec() for s in sides],
        out_shape=[jax.ShapeDtypeStruct((t, f), BF16)] + [s.out_shape() for s in sides],
        scratch_shapes=[pltpu.VMEM((tm + 2 * HALO, d), BF16)],
        compiler_params=_params(("arbitrary", "arbitrary"), 48),
        name="ffn_up",
    )(h, h, h, w, w, conv_w, conv_w, conv_b, conv_b, *[s.w for s in sides])
    return list(res)


def _attend(score_blocks, value_blocks, sink=None):
    m = jnp.max(score_blocks[0], axis=-1, keepdims=True)
    for s in score_blocks[1:]:
        m = jnp.maximum(m, jnp.max(s, axis=-1, keepdims=True))
    if sink is not None:
        m = jnp.maximum(m, sink)
    den = jnp.exp(sink - m) if sink is not None else 0.0
    out = None
    for s, v in zip(score_blocks, value_blocks):
        p = jnp.exp(s - m)
        den = den + jnp.sum(p, axis=-1, keepdims=True)
        o = _dot(p.astype(BF16), v)
        out = o if out is None else out + o
    return out / den


def _scaled_q(q):
    return (q * ATT_SCALE).astype(BF16)


CTX_SEQS_PER_STEP = 4


def _ctx_attn_kernel(*refs, n_heads, n_kv, has_sink, seq_len):
    if has_sink:
        sink_ref, q_ref, k_ref, v_ref, o_ref = refs
    else:
        q_ref, k_ref, v_ref, o_ref = refs
    grp = n_heads // n_kv

    def one_sequence(s, carry):
        rows = pl.ds(pl.multiple_of(s * seq_len, seq_len), seq_len)
        for kv in range(n_kv):
            ksl = slice(kv * HEAD_DIM, (kv + 1) * HEAD_DIM)
            k = k_ref[rows, ksl].astype(BF16)
            v = v_ref[rows, ksl].astype(BF16)
            for gi in range(grp):
                h = kv * grp + gi
                hsl = slice(h * HEAD_DIM, (h + 1) * HEAD_DIM)
                sc = _dot_t(q_ref[rows, hsl], k)
                sink = sink_ref[h] if has_sink else None
                o_ref[rows, hsl] = _attend([sc], [v], sink).astype(BF16)
        return carry

    lax.fori_loop(0, CTX_SEQS_PER_STEP, one_sequence, 0)


def _ctx_attn(q, k, v, n_seq, seq_len, n_heads, n_kv, sink=None):
    qw = n_heads * HEAD_DIM
    kw = n_kv * HEAD_DIM
    has_sink = sink is not None
    assert n_seq % CTX_SEQS_PER_STEP == 0
    blk = CTX_SEQS_PER_STEP * seq_len
    in_specs = [
        pl.BlockSpec((blk, qw), lambda b: (b, 0)),
        pl.BlockSpec((blk, kw), lambda b: (b, 0)),
        pl.BlockSpec((blk, kw), lambda b: (b, 0)),
    ]
    args = [q, k, v]
    if has_sink:
        in_specs = [pl.BlockSpec(memory_space=pltpu.SMEM)] + in_specs
        args = [sink] + args
    return pl.pallas_call(
        functools.partial(_ctx_attn_kernel, n_heads=n_heads, n_kv=n_kv, has_sink=has_sink, seq_len=seq_len),
        grid=(n_seq // CTX_SEQS_PER_STEP,),
        in_specs=in_specs,
        out_specs=pl.BlockSpec((blk, qw), lambda b: (b, 0)),
        out_shape=jax.ShapeDtypeStruct((n_seq * seq_len, qw), BF16),
        compiler_params=_params(("arbitrary",), 40),
        name="ctx_attn",
    )(*args)


NA_QROWS = 8
NA_KROWS = 16
NA_QBLK = NA_QROWS * GRID_W
NA_KBLK = NA_KROWS * GRID_W
NA_HEADS_PER_STEP = 4


def _na_key_row0(blk, rows):
    return np.clip(blk * NA_QROWS - NA_WIN_R // 2, 0, rows - NA_KROWS)


def _na_row_offsets(rows):
    nblk = rows // NA_QROWS
    a_idx = np.full((3, NA_QROWS, NA_KROWS), -1, np.int32)
    for v, blk in enumerate((0, 1, nblk - 1)):
        kr0 = _na_key_row0(blk, rows)
        for ql in range(NA_QROWS):
            qr = blk * NA_QROWS + ql
            start = np.clip(qr - NA_WIN_R // 2, 0, rows - NA_WIN_R)
            for kl in range(NA_KROWS):
                kr = kr0 + kl
                if start <= kr < start + NA_WIN_R:
                    a_idx[v, ql, kl] = kr - qr + NA_WIN_R - 1
    return a_idx


def _na_bias_kernel(rpb_ref, o_ref, *, a_idx):
    h = pl.program_id(0)
    n_dr = 2 * NA_WIN_R - 1
    n_dc = 2 * NA_WIN_C - 1
    qc = lax.broadcasted_iota(jnp.int32, (GRID_W, GRID_W), 0)
    kc = lax.broadcasted_iota(jnp.int32, (GRID_W, GRID_W), 1)
    col_start = jnp.clip(qc - NA_WIN_C // 2, 0, GRID_W - NA_WIN_C)
    col_ok = (kc >= col_start) & (kc < col_start + NA_WIN_C)
    dc = jnp.clip(kc - qc + NA_WIN_C - 1, 0, n_dc - 1)
    is_dc = [dc == j for j in range(n_dc)]
    masked = jnp.full((GRID_W, GRID_W), NEG_INF, F32)
    tiles = []
    for a in range(n_dr):
        t = jnp.zeros((GRID_W, GRID_W), F32)
        for j in range(n_dc):
            t = jnp.where(is_dc[j], rpb_ref[(h * n_dr + a) * n_dc + j], t)
        tiles.append(jnp.where(col_ok, t, NEG_INF))
    for v in range(a_idx.shape[0]):
        for ql in range(NA_QROWS):
            for kl in range(NA_KROWS):
                a = int(a_idx[v, ql, kl])
                o_ref[v, 0, ql * GRID_W:(ql + 1) * GRID_W, kl * GRID_W:(kl + 1) * GRID_W] = (
                    tiles[a] if a >= 0 else masked)


def _na_bias_tables(rpb, rows):
    h = rpb.shape[0]
    a_idx = _na_row_offsets(rows)
    return pl.pallas_call(
        functools.partial(_na_bias_kernel, a_idx=a_idx),
        grid=(h,),
        in_specs=[pl.BlockSpec(memory_space=pltpu.SMEM)],
        out_specs=pl.BlockSpec((3, 1, NA_QBLK, NA_KBLK), lambda hh: (0, hh, 0, 0)),
        out_shape=jax.ShapeDtypeStruct((3, h, NA_QBLK, NA_KBLK), F32),
        compiler_params=_params(("arbitrary",), 32),
        name="na_bias",
    )(rpb.astype(F32).reshape(-1))


def _na_kernel(q_ref, k_ref, v_ref, kc_ref, vc_ref, bias_ref, o_ref, *, rows):
    i = pl.program_id(2)
    kr0 = jnp.clip(i * NA_QROWS - NA_WIN_R // 2, 0, rows - NA_KROWS)
    k0 = pl.multiple_of(kr0 * GRID_W, GRID_W * (NA_WIN_R // 2))
    for hh in range(NA_HEADS_PER_STEP):
        hsl = slice(hh * HEAD_DIM, (hh + 1) * HEAD_DIM)
        q = q_ref[:, hsl]
        kw = k_ref[pl.ds(k0, NA_KBLK), hsl]
        vw = v_ref[pl.ds(k0, NA_KBLK), hsl]
        s_loc = _dot_t(q, kw) + bias_ref[0, hh]
        s_ctx = _dot_t(q, kc_ref[0, :, hsl].astype(BF16))
        o_ref[:, hsl] = _attend([s_loc, s_ctx], [vw, vc_ref[0, :, hsl].astype(BF16)]).astype(BF16)


def _na_attn(q, k, v, cache_k, cache_v, bias, n_ctx, n_lat, lat_len):
    rows = lat_len // GRID_W
    nblk = lat_len // NA_QBLK
    assert n_ctx % NA_QBLK == 0
    qb0 = n_ctx // NA_QBLK
    hps = NA_HEADS_PER_STEP
    hw = hps * HEAD_DIM
    hg = NA_HEADS // hps
    c = cache_k.shape[1]
    variant = lambda i: jnp.where(i == 0, 0, jnp.where(i == nblk - 1, 2, 1))
    return pl.pallas_call(
        functools.partial(_na_kernel, rows=rows),
        grid=(n_lat, hg, nblk),
        in_specs=[
            pl.BlockSpec((NA_QBLK, hw), lambda b, hh, i: (qb0 + b * nblk + i, hh)),
            pl.BlockSpec((lat_len, hw), lambda b, hh, i: (b, hh)),
            pl.BlockSpec((lat_len, hw), lambda b, hh, i: (b, hh)),
            pl.BlockSpec((1, c, hw), lambda b, hh, i: (b, 0, hh)),
            pl.BlockSpec((1, c, hw), lambda b, hh, i: (b, 0, hh)),
            pl.BlockSpec((1, hps, NA_QBLK, NA_KBLK), lambda b, hh, i: (variant(i), hh, 0, 0)),
        ],
        out_specs=pl.BlockSpec((NA_QBLK, hw), lambda b, hh, i: (b * nblk + i, hh)),
        out_shape=jax.ShapeDtypeStruct((n_lat * lat_len, NA_HEADS * HEAD_DIM), BF16),
        compiler_params=_params(("arbitrary", "arbitrary", "arbitrary"), 56),
        name="na_attn",
    )(q, k, v, cache_k, cache_v, bias)


SWA_QBLK = 512
SWA_KBLK = SWA_QBLK + 2 * SWA_WINDOW


def _rope_tables(length):
    half = HEAD_DIM // 2
    freqs = ROPE_BASE ** (-jnp.arange(0, half, 2, dtype=F32) / half)
    t = jnp.arange(length)
    ang_r = (t // GRID_W).astype(F32)[:, None] * freqs[None, :]
    ang_c = (t % GRID_W).astype(F32)[:, None] * freqs[None, :]
    ang = jnp.concatenate([ang_r, ang_r, ang_c, ang_c], axis=1)
    sign = np.concatenate([-np.ones(half // 2), np.ones(half // 2)] * 2).astype(np.float32)
    return jnp.cos(ang), jnp.sin(ang) * jnp.asarray(sign)[None, :]


def _rope(x, cos, sin_signed):
    lane = lax.broadcasted_iota(jnp.int32, x.shape, 1)
    quarter = HEAD_DIM // 4
    lower = (lane & (2 * quarter - 1)) < quarter
    swapped = jnp.where(lower, pltpu.roll(x, HEAD_DIM - quarter, axis=1), pltpu.roll(x, quarter, axis=1))
    return x * cos + swapped * sin_signed


def _swa_kernel(sink_ref, q_ref, k_ref, v_ref, kc_ref, vc_ref, cos_ref, sin_ref, o_ref, *, lat_len):
    kv = pl.program_id(1)
    i = pl.program_id(2)
    grp = SWA_HEADS // SWA_KV_HEADS
    q0 = pl.multiple_of(i * SWA_QBLK, SWA_QBLK)
    k0 = pl.multiple_of(jnp.clip(i * SWA_QBLK - SWA_WINDOW, 0, lat_len - SWA_KBLK), SWA_WINDOW)
    kw = _rope(k_ref[pl.ds(k0, SWA_KBLK), :], cos_ref[pl.ds(k0, SWA_KBLK), :],
               sin_ref[pl.ds(k0, SWA_KBLK), :]).astype(BF16)
    vw = v_ref[pl.ds(k0, SWA_KBLK), :].astype(BF16)
    kc = kc_ref[0].astype(BF16)
    vc = vc_ref[0].astype(BF16)
    qpos = q0 + lax.broadcasted_iota(jnp.int32, (SWA_QBLK, SWA_KBLK), 0)
    kpos = k0 + lax.broadcasted_iota(jnp.int32, (SWA_QBLK, SWA_KBLK), 1)
    band = jnp.where(jnp.abs(qpos - kpos) <= SWA_WINDOW, 0.0, NEG_INF)
    cos_q = cos_ref[pl.ds(q0, SWA_QBLK), :]
    sin_q = sin_ref[pl.ds(q0, SWA_QBLK), :]
    for gi in range(grp):
        hsl = slice(gi * HEAD_DIM, (gi + 1) * HEAD_DIM)
        q = _scaled_q(_rope(q_ref[:, hsl], cos_q, sin_q))
        s_loc = _dot_t(q, kw) + band
        s_ctx = _dot_t(q, kc)
        sink = sink_ref[kv * grp + gi]
        o_ref[:, hsl] = _attend([s_loc, s_ctx], [vw, vc], sink).astype(BF16)


def _swa_attn(q, kv, cache_k, cache_v, sink, cos, sin_signed, n_lat, lat_len):
    grp = SWA_HEADS // SWA_KV_HEADS
    gw = grp * HEAD_DIM
    nblk = lat_len // SWA_QBLK
    c = cache_k.shape[1]
    return pl.pallas_call(
        functools.partial(_swa_kernel, lat_len=lat_len),
        grid=(n_lat, SWA_KV_HEADS, nblk),
        in_specs=[
            pl.BlockSpec(memory_space=pltpu.SMEM),
            pl.BlockSpec((SWA_QBLK, gw), lambda b, kv, i: (b * nblk + i, kv)),
            pl.BlockSpec((lat_len, HEAD_DIM), lambda b, kv, i: (b, kv)),
            pl.BlockSpec((lat_len, HEAD_DIM), lambda b, kv, i: (b, SWA_KV_HEADS + kv)),
            pl.BlockSpec((1, c, HEAD_DIM), lambda b, kv, i: (b, 0, kv)),
            pl.BlockSpec((1, c, HEAD_DIM), lambda b, kv, i: (b, 0, kv)),
            pl.BlockSpec((lat_len, HEAD_DIM), lambda b, kv, i: (0, 0)),
            pl.BlockSpec((lat_len, HEAD_DIM), lambda b, kv, i: (0, 0)),
        ],
        out_specs=pl.BlockSpec((SWA_QBLK, gw), lambda b, kv, i: (b * nblk + i, kv)),
        out_shape=jax.ShapeDtypeStruct((n_lat * lat_len, SWA_HEADS * HEAD_DIM), BF16),
        compiler_params=_params(("arbitrary", "arbitrary", "arbitrary"), 48),
        name="swa_attn",
    )(sink, q, kv, kv, cache_k, cache_v, cos, sin_signed)


def _s5_disc_kernel(are_ref, aim_ref, ldt_ref, bre_ref, bim_ref, abre_ref, abim_ref, bbre_ref, bbim_ref):
    a_re = are_ref[...]
    a_im = aim_ref[...]
    dt = jnp.exp(ldt_ref[...])
    mag = jnp.exp(a_re * dt)
    ang = a_im * dt
    ab_re = mag * jnp.cos(ang)
    ab_im = mag * jnp.sin(ang)
    den = a_re * a_re + a_im * a_im
    n_re = ab_re - 1.0
    f_re = (n_re * a_re + ab_im * a_im) / den
    f_im = (ab_im * a_re - n_re * a_im) / den
    b_re = bre_ref[...]
    b_im = bim_ref[...]
    abre_ref[...] = ab_re
    abim_ref[...] = ab_im
    bbre_ref[...] = f_re * b_re - f_im * b_im
    bbim_ref[...] = f_re * b_im + f_im * b_re


def _s5_discretise(a_re, a_im, log_dt, b_re, b_im):
    nd, g, p, c = b_re.shape
    shape2 = (nd * g, p * c)
    expand = lambda a: jnp.broadcast_to(a[..., None], (nd, g, p, c)).reshape(shape2)
    ldt = jnp.broadcast_to(log_dt[:, :, None, None], (nd, g, p, c)).reshape(shape2)
    out = pl.pallas_call(
        _s5_disc_kernel,
        out_shape=[jax.ShapeDtypeStruct(shape2, F32)] * 4,
        compiler_params=_params(None, 16),
        name="s5_discretise",
    )(expand(a_re), expand(a_im), ldt, b_re.reshape(shape2), b_im.reshape(shape2))
    ab_re, ab_im, bb_re, bb_im = [o.reshape(nd, g, p, c) for o in out]
    return ab_re[..., 0], ab_im[..., 0], bb_re, bb_im


def _block_diag(m, gb):
    g, a, b = m.shape
    m = m.reshape(g // gb, gb, a, b)
    eye = jnp.eye(gb, dtype=m.dtype)
    out = m[:, :, :, None, :] * eye[None, :, None, :, None]
    return out.reshape(g // gb, gb * a, gb * b)


def _cmul_add(ar, ai, xr, xi, br, bi):
    return ar * xr - ai * xi + br, ar * xi + ai * xr + bi


def _s5_kernel(*refs, latent):
    if latent:
        (u_ref, bbr_ref, bbi_ref, ccr_ref, cci_ref, ar_ref, ai_ref, d_ref, h0_ref,
         y_ref, xr_s, xi_s, cr_s, ci_s) = refs
    else:
        (u_ref, bbr_ref, bbi_ref, ccr_ref, cci_ref, ar_ref, ai_ref, d_ref,
         y_ref, fin_ref, xr_s, xi_s) = refs
    rows, steps = S5_ROWS, S5_STEPS
    u = u_ref[0]
    ub = u.astype(BF16)
    y = d_ref[0] * u
    for dirn in range(2):
        reverse = dirn == 1
        xr_s[...] = _dot(ub, bbr_ref[dirn, 0])
        xi_s[...] = _dot(ub, bbi_ref[dirn, 0])
        a_r1 = ar_ref[dirn, 0]
        a_i1 = ai_ref[dirn, 0]
        a_r = jnp.broadcast_to(a_r1, (rows, S5_GB_ST))
        a_i = jnp.broadcast_to(a_i1, (rows, S5_GB_ST))

        def time_rows(s):
            t = (steps - 1 - s) if reverse else s
            if isinstance(t, int):
                return pl.ds(t * rows, rows)
            return pl.ds(pl.multiple_of(t * rows, rows), rows)

        first = time_rows(0)
        final = time_rows(steps - 1)

        def scan_body(s, carry):
            pr, pi = carry
            sl = time_rows(s)
            nr, ni = _cmul_add(a_r, a_i, pr, pi, xr_s[sl, :], xi_s[sl, :])
            xr_s[sl, :] = nr
            xi_s[sl, :] = ni
            return nr, ni

        lax.fori_loop(1, steps, scan_body, (xr_s[first, :], xi_s[first, :]), unroll=4)

        if not latent:
            fin_ref[0, dirn, 0] = xr_s[final, :]
            fin_ref[0, dirn, 1] = xi_s[final, :]
        else:
            p_r, p_i = a_r1, a_i1
            for _ in range(int(math.log2(steps))):
                p_r, p_i = p_r * p_r - p_i * p_i, 2.0 * p_r * p_i
            assert 1 << int(math.log2(steps)) == steps
            e_r = xr_s[final, :]
            e_i = xi_s[final, :]
            c_r = h0_ref[0, 2 * dirn]
            c_i = h0_ref[0, 2 * dirn + 1]
            order = range(rows - 1, -1, -1) if reverse else range(rows)
            for j in order:
                cr_s[j:j + 1, :] = c_r
                ci_s[j:j + 1, :] = c_i
                c_r, c_i = _cmul_add(p_r, p_i, c_r, c_i, e_r[j:j + 1], e_i[j:j + 1])
            zero = jnp.zeros((rows, S5_GB_ST), F32)
            d_r, d_i = _cmul_add(a_r, a_i, cr_s[...], ci_s[...], zero, zero)

            def fix_body(s, carry):
                dr, di = carry
                sl = time_rows(s)
                xr_s[sl, :] = xr_s[sl, :] + dr
                xi_s[sl, :] = xi_s[sl, :] + di
                return _cmul_add(a_r, a_i, dr, di, zero, zero)

            lax.fori_loop(0, steps, fix_body, (d_r, d_i), unroll=4)

        y = y + _dot(xr_s[...].astype(BF16), ccr_ref[dirn, 0]) - _dot(xi_s[...].astype(BF16), cci_ref[dirn, 0])
    y_ref[0] = y


def _s5_scan(u_tm, tile0, tiles, bbr, bbi, ccr, cci, ab_re, ab_im, d, h0=None):
    _, tr, w = u_tm.shape
    assert tr == S5_STEPS * S5_ROWS
    nb = w // S5_GB_IN
    latent = h0 is not None
    in_specs = [
        pl.BlockSpec((1, tr, S5_GB_IN), lambda t, gb: (tile0 + t, 0, gb)),
        pl.BlockSpec((2, 1, S5_GB_IN, S5_GB_ST), lambda t, gb: (0, gb, 0, 0)),
        pl.BlockSpec((2, 1, S5_GB_IN, S5_GB_ST), lambda t, gb: (0, gb, 0, 0)),
        pl.BlockSpec((2, 1, S5_GB_ST, S5_GB_IN), lambda t, gb: (0, gb, 0, 0)),
        pl.BlockSpec((2, 1, S5_GB_ST, S5_GB_IN), lambda t, gb: (0, gb, 0, 0)),
        pl.BlockSpec((2, 1, 1, S5_GB_ST), lambda t, gb: (0, gb, 0, 0)),
        pl.BlockSpec((2, 1, 1, S5_GB_ST), lambda t, gb: (0, gb, 0, 0)),
        pl.BlockSpec((1, 1, S5_GB_IN), lambda t, gb: (gb, 0, 0)),
    ]
    args = [u_tm, bbr, bbi, ccr, cci, ab_re, ab_im, d]
    y_spec = pl.BlockSpec((1, tr, S5_GB_IN), lambda t, gb: (t, 0, gb))
    y_shape = jax.ShapeDtypeStruct((tiles, tr, w), F32)
    scratch = [pltpu.VMEM((tr, S5_GB_ST), F32), pltpu.VMEM((tr, S5_GB_ST), F32)]
    if latent:
        in_specs.append(pl.BlockSpec((1, 4, 1, S5_GB_ST), lambda t, gb: (t, 0, 0, gb)))
        args.append(h0)
        out_specs, out_shape = y_spec, y_shape
        scratch += [pltpu.VMEM((S5_ROWS, S5_GB_ST), F32), pltpu.VMEM((S5_ROWS, S5_GB_ST), F32)]
    else:
        out_specs = [y_spec, pl.BlockSpec((1, 2, 2, S5_ROWS, S5_GB_ST), lambda t, gb: (t, 0, 0, 0, gb))]
        out_shape = [y_shape, jax.ShapeDtypeStruct((tiles, 2, 2, S5_ROWS, nb * S5_GB_ST), F32)]
    return pl.pallas_call(
        functools.partial(_s5_kernel, latent=latent),
        grid=(tiles, nb),
        in_specs=in_specs,
        out_specs=out_specs,
        out_shape=out_shape,
        scratch_shapes=scratch,
        compiler_params=_params(("arbitrary", "arbitrary"), 48),
        name="s5_scan_lat" if latent else "s5_scan_ctx",
    )(*args)


def _glu_kernel(*refs, n_y, first_tiles):
    y_refs = refs[:n_y]
    w_ref, b_ref, o_ref = refs[n_y:]

    def run(y_ref):
        y = y_ref[...]
        g = 0.5 * y * (1.0 + jnp.tanh(math.sqrt(2.0 / math.pi) * (y + 0.044715 * (y * y * y))))
        z = _dot(g.astype(BF16), w_ref[...]) + b_ref[...]
        o_ref[...] = (g * _sigmoid(z)).astype(BF16)

    _select_rows(y_refs, pl.program_id(0), first_tiles, run)


def _glu(y, w, b, tm=1024):
    t, n = y.rows, y.width
    return pl.pallas_call(
        functools.partial(_glu_kernel, n_y=y.n_refs(), first_tiles=y.first_tiles(tm)),
        grid=(t // tm,),
        in_specs=y.specs(tm) + [
            pl.BlockSpec((n, n), lambda i: (0, 0)),
            pl.BlockSpec((1, n), lambda i: (0, 0)),
        ],
        out_specs=pl.BlockSpec((tm, n), lambda i: (i, 0)),
        out_shape=jax.ShapeDtypeStruct((t, n), BF16),
        compiler_params=_params(("arbitrary",), 32),
        name="s5_glu",
    )(*y.arrays, w, b.reshape(1, n))


def _s5_mixer(u, p, state, n_ctx_seq, ctx_len, n_lat, lat_len):
    t, w = u.shape
    g = w // S5_GROUP
    assert ctx_len == S5_STEPS and lat_len == S5_STEPS * S5_ROWS and n_ctx_seq % S5_ROWS == 0
    ab_re, ab_im, bb_re, bb_im = _s5_discretise(p["a_re"], p["a_im"], p["log_dt"], p["b_re"], p["b_im"])
    nb = g // S5_GB
    blocks = lambda m: jnp.stack([_block_diag(m[d].transpose(0, 2, 1), S5_GB) for d in range(2)]).astype(BF16)
    bbr, bbi = blocks(bb_re), blocks(bb_im)
    ccr, cci = blocks(p["c_re"].astype(F32)), blocks(p["c_im"].astype(F32))
    a_r = ab_re.reshape(2, nb, 1, S5_GB_ST)
    a_i = ab_im.reshape(2, nb, 1, S5_GB_ST)
    d = p["d"].astype(F32).reshape(nb, 1, S5_GB_IN)
    tiles = t // (S5_ROWS * S5_STEPS)
    ctx_tiles = n_ctx_seq // S5_ROWS
    tr = S5_STEPS * S5_ROWS
    u_tm = u.reshape(tiles, S5_ROWS, S5_STEPS, w).transpose(0, 2, 1, 3).reshape(tiles, tr, w)
    y_ctx, fin = _s5_scan(u_tm, 0, ctx_tiles, bbr, bbi, ccr, cci, a_r, a_i, d)
    h0 = state.astype(F32).transpose(0, 1, 4, 2, 3).reshape(n_lat, 4, 1, g * S5_STATE)
    y_lat = _s5_scan(u_tm, ctx_tiles, tiles - ctx_tiles, bbr, bbi, ccr, cci, a_r, a_i, d, h0=h0)
    y = _Rows(y_ctx.reshape(ctx_tiles * tr, w), y_lat.reshape((tiles - ctx_tiles) * tr, w))
    o = _glu(y, p["glu_w"].astype(BF16), p["glu_b"].astype(F32))
    o = o.reshape(tiles, S5_STEPS, S5_ROWS, w).transpose(0, 2, 1, 3).reshape(t, w)
    fin = fin.transpose(0, 3, 1, 4, 2).reshape(n_ctx_seq, 2, g, S5_STATE, 2)
    return o, fin


def kernel(x_prompt, x_sample, c, cache_na_k, cache_na_v, state_s5, cache_swa_k, cache_swa_v, c_ctx, norm_mix, norm_ffn, ada_w, ada_b, ab_w_in, ab_w_out, na_rpb, s5_a_re, s5_a_im, s5_log_dt, s5_b_re, s5_b_im, s5_c_re, s5_c_im, s5_d, s5_glu_w, s5_glu_b, swa_w_in, swa_w_out, swa_sink, ffn_w_up, ffn_conv_w, ffn_conv_b, ffn_w_down, final_norm):
    bp, lp, d = x_prompt.shape
    bs, ls, _ = x_sample.shape
    depth = ada_w.shape[0]
    n_ctx = bp * lp
    n_lat = bs * ls
    na_w = NA_HEADS * HEAD_DIM
    swa_q = SWA_HEADS * HEAD_DIM
    swa_kv = SWA_KV_HEADS * HEAD_DIM
    assert depth == 2, "layer pattern below is written for one A/B layer followed by one C layer"

    cond = jnp.concatenate([c_ctx[None, :], c, jnp.zeros((SUBLANES - 1 - bs, d), F32)], axis=0)
    mods = _adaln(cond, ada_w, ada_b).reshape(depth, SUBLANES, 6, 1, d)
    cos, sin_signed = _rope_tables(ls)
    tn = 1024
    assert na_w == tn and swa_q == 2 * tn and 2 * swa_kv == tn

    x0 = _Rows(x_prompt.reshape(n_ctx, d), x_sample.reshape(n_lat, d))
    q, k_ctx, k_lat, v_ctx, v_lat, u, w_up0 = _proj(
        _norm_mod(x0, mods[0], norm_mix[0], n_ctx, ls), _to_bf16(ab_w_in[0]),
        [_ProjOut(0, 1, "all", BF16, scale=ATT_SCALE),
         _ProjOut(1, 2, "ctx", F32), _ProjOut(1, 2, "lat", BF16),
         _ProjOut(2, 3, "ctx", F32), _ProjOut(2, 3, "lat", BF16),
         _ProjOut(3, 4, "all", F32)], n_ctx, tn=tn, side_weights=[(ffn_w_up, 0)])
    att_ctx = _ctx_attn(q, k_ctx, v_ctx, bp, lp, NA_HEADS, NA_HEADS)
    bias = _na_bias_tables(na_rpb[0], ls // GRID_W)
    att_lat = _na_attn(q, k_lat, v_lat, cache_na_k[:, 0].reshape(bs, -1, na_w),
                       cache_na_v[:, 0].reshape(bs, -1, na_w), bias, n_ctx, bs, ls)
    s5p = dict(a_re=s5_a_re[0], a_im=s5_a_im[0], log_dt=s5_log_dt[0], b_re=s5_b_re[0], b_im=s5_b_im[0],
               c_re=s5_c_re[0], c_im=s5_c_im[0], d=s5_d[0], glu_w=s5_glu_w[0], glu_b=s5_glu_b[0])
    s5o, s5_fin = _s5_mixer(u, s5p, state_s5[:, 0], bp, lp, bs, ls)
    x, h = _outproj([_Rows(att_ctx, att_lat), _Rows(s5o)], _to_bf16(ab_w_out[0]), x0, mods[0], norm_ffn[0],
                    n_ctx, ls)
    na_k = k_ctx.reshape(bp, 1, lp, NA_HEADS, HEAD_DIM)
    na_v = v_ctx.reshape(bp, 1, lp, NA_HEADS, HEAD_DIM)
    act, w_up1, w_down0, w_down1 = _ffn_up(h, w_up0, ffn_conv_w[0], ffn_conv_b[0], n_ctx, lp, ls,
                                           side_weights=[(ffn_w_up, 1), (ffn_w_down, 0), (ffn_w_down, 1)])
    x, h = _ffn_down(act, w_down0, x, mods[0], mods[1], norm_mix[1], n_ctx, ls)

    q_ctx, q_lat, k_ctx, v_ctx, kv_lat = _proj(
        h, _to_bf16(swa_w_in[0]),
        [_ProjOut(0, 2, "ctx", BF16, scale=ATT_SCALE), _ProjOut(0, 2, "lat", F32),
         _ProjOut(2, 3, "ctx", F32, cols=(0, swa_kv)), _ProjOut(2, 3, "ctx", F32, cols=(swa_kv, 2 * swa_kv)),
         _ProjOut(2, 3, "lat", F32)], n_ctx, tn=tn)
    sink = swa_sink[0].astype(F32)
    att_ctx = _ctx_attn(q_ctx, k_ctx, v_ctx, bp, lp, SWA_HEADS, SWA_KV_HEADS, sink=sink)
    att_lat = _swa_attn(q_lat, kv_lat, cache_swa_k[:, 0].reshape(bs, -1, swa_kv),
                        cache_swa_v[:, 0].reshape(bs, -1, swa_kv), sink, cos, sin_signed, bs, ls)
    x, h = _outproj([_Rows(att_ctx, att_lat)], _to_bf16(swa_w_out[0]), _Rows(x), mods[1], norm_ffn[1], n_ctx, ls)
    swa_k = k_ctx.reshape(bp, 1, lp, SWA_KV_HEADS, HEAD_DIM)
    swa_v = v_ctx.reshape(bp, 1, lp, SWA_KV_HEADS, HEAD_DIM)
    (act,) = _ffn_up(h, w_up1, ffn_conv_w[1], ffn_conv_b[1], n_ctx, lp, ls)
    y_p = _ffn_down(act, w_down1, x, mods[1], None, final_norm, n_ctx, ls, row0=0, n_rows=n_ctx)
    y_s = _ffn_down(act, w_down1, x, mods[1], None, final_norm, n_ctx, ls, row0=n_ctx, n_rows=n_lat)

    return (y_p.reshape(bp, lp, d), y_s.reshape(bs, ls, d), na_k, na_v, s5_fin[:, None], swa_k, swa_v)
```

```python
import functools
import math

import jax
import jax.numpy as jnp
import numpy as np
from jax import lax
from jax.experimental import pallas as pl
from jax.experimental.pallas import tpu as pltpu

F32 = jnp.float32
BF16 = jnp.bfloat16

GRID_W = 64
HEAD_DIM = 128
NA_HEADS = 8
NA_WIN_R = 8
NA_WIN_C = 16
S5_GROUP = 16
S5_STATE = 64
SWA_HEADS = 16
SWA_KV_HEADS = 4
SWA_WINDOW = 128
ROPE_BASE = 10000.0
EPS = 1e-6
NEG_INF = -1e30
ATT_SCALE = HEAD_DIM ** -0.5

VMEM_PHYSICAL_MIB = 64
SUBLANES = 8
BF16_SUBLANES = 16
LANES = 128

S5_STEPS = 256
S5_ROWS = 16
S5_GB = 8
S5_GB_IN = S5_GB * S5_GROUP
S5_GB_ST = S5_GB * S5_STATE


def _params(sem, vmem_mib):
    assert vmem_mib <= VMEM_PHYSICAL_MIB
    if sem is None:
        return pltpu.CompilerParams(vmem_limit_bytes=vmem_mib << 20)
    return pltpu.CompilerParams(dimension_semantics=sem, vmem_limit_bytes=vmem_mib << 20)


def _dot(a, b):
    return jnp.dot(a, b, preferred_element_type=F32)


def _dot_t(a, b):
    return lax.dot_general(a, b, (((1,), (1,)), ((), ())), preferred_element_type=F32)


def _sigmoid(x):
    return 1.0 / (1.0 + jnp.exp(-x))


class _Rows:
    def __init__(self, *arrays):
        assert len(arrays) in (1, 2)
        self.arrays = arrays
        self.width = arrays[0].shape[1]
        self.dtype = arrays[0].dtype
        self.rows = sum(a.shape[0] for a in arrays)

    def specs(self, tm, row_of_grid=lambda *g: g[0]):
        if len(self.arrays) == 1:
            return [pl.BlockSpec((tm, self.width), lambda *g: (row_of_grid(*g), 0))]
        na = self.arrays[0].shape[0] // tm
        return [
            pl.BlockSpec((tm, self.width), lambda *g: (jnp.minimum(row_of_grid(*g), na - 1), 0)),
            pl.BlockSpec((tm, self.width), lambda *g: (jnp.maximum(row_of_grid(*g) - na, 0), 0)),
        ]

    def n_refs(self):
        return len(self.arrays)

    def first_tiles(self, tm):
        return self.arrays[0].shape[0] // tm


def _select_rows(refs, tile, first_tiles, consume):
    if len(refs) == 1:
        consume(refs[0])
        return

    @pl.when(tile < first_tiles)
    def _():
        consume(refs[0])

    @pl.when(tile >= first_tiles)
    def _():
        consume(refs[1])


def _staged_rows(refs, tile, first_tiles, stage_ref):
    if len(refs) == 1:
        return refs[0]

    def copy(ref):
        stage_ref[...] = ref[...]
    _select_rows(refs, tile, first_tiles, copy)
    return stage_ref


def _adaln_kernel(c_ref, w_ref, b_ref, o_ref):
    c = c_ref[...]
    s = (c * _sigmoid(c)).astype(BF16)
    o_ref[0] = _dot(s, w_ref[0].astype(BF16)) + b_ref[0]


def _adaln(cond8, ada_w, ada_b):
    depth, d, n = ada_w.shape
    tn = 1024
    return pl.pallas_call(
        _adaln_kernel,
        grid=(depth, n // tn),
        in_specs=[
            pl.BlockSpec((SUBLANES, d), lambda l, j: (0, 0)),
            pl.BlockSpec((1, d, tn), lambda l, j: (l, 0, j)),
            pl.BlockSpec((1, 1, tn), lambda l, j: (l, 0, j)),
        ],
        out_specs=pl.BlockSpec((1, SUBLANES, tn), lambda l, j: (l, 0, j)),
        out_shape=jax.ShapeDtypeStruct((depth, SUBLANES, n), F32),
        compiler_params=_params(("arbitrary", "arbitrary"), 40),
        name="adaln",
    )(cond8, ada_w, ada_b.reshape(depth, 1, n))


MOD_SHIFT1, MOD_SCALE1, MOD_GATE1, MOD_SHIFT2, MOD_SCALE2, MOD_GATE2 = range(6)


def _group_of(row0, n_ctx, lat_len):
    return jnp.maximum((row0 - n_ctx) // lat_len + 1, 0)


def _mod_spec(which, tm, n_ctx, lat_len, d):
    return pl.BlockSpec((1, 1, 1, d), lambda *g: (_group_of(g[0] * tm, n_ctx, lat_len), which, 0, 0))


def _norm_mod_kernel(*refs, n_x, first_tiles):
    x_refs = refs[:n_x]
    sh_ref, sc_ref, g_ref, o_ref = refs[n_x:]

    def run(x_ref):
        d = x_ref.shape[1]
        ss = None
        for c0 in range(0, d, COL_CHUNK):
            x = x_ref[:, c0:c0 + COL_CHUNK]
            s1 = jnp.sum(x * x, axis=-1, keepdims=True)
            ss = s1 if ss is None else ss + s1
        inv = lax.rsqrt(ss * (1.0 / d) + EPS)
        for c0 in range(0, d, COL_CHUNK):
            cols = slice(c0, c0 + COL_CHUNK)
            y = x_ref[:, cols] * inv * g_ref[:, cols]
            o_ref[:, cols] = (y * (1.0 + sc_ref[0, 0, :, cols]) + sh_ref[0, 0, :, cols]).astype(BF16)

    _select_rows(x_refs, pl.program_id(0), first_tiles, run)


def _norm_mod(x, mod, g, n_ctx, lat_len, tm=512):
    t, d = x.rows, x.width
    return pl.pallas_call(
        functools.partial(_norm_mod_kernel, n_x=x.n_refs(), first_tiles=x.first_tiles(tm)),
        grid=(t // tm,),
        in_specs=x.specs(tm) + [
            _mod_spec(MOD_SHIFT1, tm, n_ctx, lat_len, d),
            _mod_spec(MOD_SCALE1, tm, n_ctx, lat_len, d),
            pl.BlockSpec((1, d), lambda i: (0, 0)),
        ],
        out_specs=pl.BlockSpec((tm, d), lambda i: (i, 0)),
        out_shape=jax.ShapeDtypeStruct((t, d), BF16),
        compiler_params=_params(("arbitrary",), 40),
        name="norm_mod",
    )(*x.arrays, mod, mod, g.reshape(1, d))


class _ProjOut:
    def __init__(self, j0, j1, part, dtype, scale=None, cols=None):
        assert part in ("all", "ctx", "lat") and (cols is None or j1 == j0 + 1)
        self.j0, self.j1, self.part, self.dtype, self.scale, self.cols = j0, j1, part, dtype, scale, cols


class _SideCast:
    def __init__(self, w_stacked, layer, grid):
        gi, gj = grid
        _, r, c = w_stacked.shape
        self.w, self.layer, self.shape = w_stacked, layer, (r, c)
        if r % (gi * gj * BF16_SUBLANES) == 0:
            self.block = (r // (gi * gj), c)
            self.index = lambda i, j: (i * gj + j, 0)
        elif r % (gi * BF16_SUBLANES) == 0 and c % (gj * LANES) == 0:
            self.block = (r // gi, c // gj)
            self.index = lambda i, j: (i, j)
        else:
            assert r % (gj * BF16_SUBLANES) == 0 and c % (gi * LANES) == 0
            self.block = (r // gj, c // gi)
            self.index = lambda i, j: (j, i)

    def in_spec(self):
        return pl.BlockSpec((None,) + self.block, lambda i, j: (self.layer,) + self.index(i, j))

    def out_spec(self):
        return pl.BlockSpec(self.block, self.index)

    def out_shape(self):
        return jax.ShapeDtypeStruct(self.shape, BF16)


def _run_side_casts(in_refs, out_refs):
    for s_ref, o_ref in zip(in_refs, out_refs):
        o_ref[...] = s_ref[...].astype(BF16)


def _proj_kernel(h_ref, w_ref, *rest, outs, ctx_tiles, n_sides):
    side_in = rest[:n_sides]
    o_refs = rest[n_sides:n_sides + len(outs)]
    side_out = rest[n_sides + len(outs):]
    _run_side_casts(side_in, side_out)
    i = pl.program_id(0)
    j = pl.program_id(1)
    for o, o_ref in zip(outs, o_refs):
        mine = (j >= o.j0) & (j < o.j1)
        if o.part == "ctx":
            mine = mine & (i < ctx_tiles)
        elif o.part == "lat":
            mine = mine & (i >= ctx_tiles)

        @pl.when(mine)
        def _(o=o, o_ref=o_ref):
            w = w_ref[...] if o.cols is None else w_ref[:, o.cols[0]:o.cols[1]]
            v = _dot(h_ref[...], w)
            if o.scale is not None:
                v = v * o.scale
            o_ref[...] = v.astype(o.dtype)


def _proj(h, w, outs, n_ctx, tm=1024, tn=1024, side_weights=()):
    t, d = h.shape
    n = w.shape[1]
    nc = n_ctx // tm
    nj = n // tn
    sides = [_SideCast(sw, layer, (t // tm, nj)) for sw, layer in side_weights]
    n_rows = {"all": t, "ctx": n_ctx, "lat": t - n_ctx}

    def block_index(o, tiles, i, j):
        col = jnp.clip(j - o.j0, 0, tiles - 1)
        if o.part == "ctx":
            return jnp.minimum(i, nc - 1), jnp.where(i < nc, col, tiles - 1)
        if o.part == "lat":
            return jnp.maximum(i - nc, 0), jnp.where(i >= nc, col, 0)
        return i, col

    out_specs, out_shape = [], []
    for o in outs:
        width = tn if o.cols is None else o.cols[1] - o.cols[0]
        tiles = o.j1 - o.j0
        out_specs.append(pl.BlockSpec((tm, width), functools.partial(block_index, o, tiles)))
        out_shape.append(jax.ShapeDtypeStruct((n_rows[o.part], width * tiles), o.dtype))
    res = pl.pallas_call(
        functools.partial(_proj_kernel, outs=outs, ctx_tiles=nc, n_sides=len(sides)),
        grid=(t // tm, nj),
        in_specs=[pl.BlockSpec((tm, d), lambda i, j: (i, 0)), pl.BlockSpec((d, tn), lambda i, j: (0, j))]
        + [s.in_spec() for s in sides],
        out_specs=out_specs + [s.out_spec() for s in sides],
        out_shape=out_shape + [s.out_shape() for s in sides],
        compiler_params=_params(("arbitrary", "arbitrary"), 62),
        name="proj",
    )(h, w, *[s.w for s in sides])
    return list(res)


def _to_bf16_kernel(w_ref, o_ref):
    o_ref[...] = w_ref[...].astype(BF16)


def _to_bf16(w, block_bytes=8 << 20):
    shape = w.shape
    w2 = w.reshape(-1, shape[-1])
    r, c = w2.shape
    tr = max(BF16_SUBLANES, min(r, (block_bytes // (4 * c)) // BF16_SUBLANES * BF16_SUBLANES))
    while r % tr:
        tr -= BF16_SUBLANES
    out = pl.pallas_call(
        _to_bf16_kernel,
        grid=(r // tr,),
        in_specs=[pl.BlockSpec((tr, c), lambda i: (i, 0))],
        out_specs=pl.BlockSpec((tr, c), lambda i: (i, 0)),
        out_shape=jax.ShapeDtypeStruct((r, c), BF16),
        compiler_params=_params(("arbitrary",), 40),
        name="to_bf16",
    )(w2)
    return out.reshape(shape)


COL_CHUNK = 512


def _residual_norm(acc_of_cols, x_of_cols, gate_ref, xo_ref, d, tm):
    ss = jnp.zeros((tm, 1), F32)
    for c0 in range(0, d, COL_CHUNK):
        cols = slice(c0, c0 + COL_CHUNK)
        xn = x_of_cols(cols) + gate_ref[0, 0, :, cols] * acc_of_cols(cols)
        xo_ref[:, cols] = xn
        ss = ss + jnp.sum(xn * xn, axis=-1, keepdims=True)
    return lax.rsqrt(ss * (1.0 / d) + EPS)


def _write_modulated(xo_ref, inv, g_ref, sh_ref, sc_ref, h_ref, d):
    for c0 in range(0, d, COL_CHUNK):
        cols = slice(c0, c0 + COL_CHUNK)
        y = xo_ref[:, cols] * inv * g_ref[:, cols]
        h_ref[:, cols] = (y * (1.0 + sc_ref[0, 0, :, cols]) + sh_ref[0, 0, :, cols]).astype(BF16)


def _outproj_kernel(*refs, part_refs, part_first, n_x, x_first, d, tm):
    pos = 0
    lhs_refs = []
    for n in part_refs:
        lhs_refs.append(refs[pos:pos + n])
        pos += n
    x_refs = refs[pos:pos + n_x]
    pos += n_x
    w_ref, gate_ref, g_ref, sh_ref, sc_ref, xo_ref, h_ref = refs[pos:pos + 7]
    stage = list(refs[pos + 7:])
    tile = pl.program_id(0)

    lhs = []
    for prefs, first in zip(lhs_refs, part_first):
        lhs.append(_staged_rows(prefs, tile, first, stage.pop(0) if len(prefs) > 1 else None))
    x_ref = _staged_rows(x_refs, tile, x_first, stage.pop(0) if n_x > 1 else None)

    def acc_of_cols(cols):
        row = 0
        acc = None
        for l_ref in lhs:
            k = l_ref.shape[1]
            part = _dot(l_ref[...], w_ref[row:row + k, cols])
            acc = part if acc is None else acc + part
            row += k
        return acc

    inv = _residual_norm(acc_of_cols, lambda cols: x_ref[:, cols], gate_ref, xo_ref, d, tm)
    _write_modulated(xo_ref, inv, g_ref, sh_ref, sc_ref, h_ref, d)


def _outproj(parts, w, x, mod, g_next, n_ctx, lat_len, tm=512):
    t, d = x.rows, x.width
    in_specs, args, scratch = [], [], []
    for p in parts:
        in_specs += p.specs(tm)
        args += list(p.arrays)
    in_specs += x.specs(tm)
    args += list(x.arrays)
    in_specs += [
        pl.BlockSpec(w.shape, lambda i: (0, 0), pipeline_mode=pl.Buffered(1)),
        _mod_spec(MOD_GATE1, tm, n_ctx, lat_len, d),
        pl.BlockSpec((1, d), lambda i: (0, 0)),
        _mod_spec(MOD_SHIFT2, tm, n_ctx, lat_len, d),
        _mod_spec(MOD_SCALE2, tm, n_ctx, lat_len, d),
    ]
    args += [w, mod, g_next.reshape(1, d), mod, mod]
    scratch = [pltpu.VMEM((tm, p.width), p.dtype) for p in parts if p.n_refs() > 1]
    if x.n_refs() > 1:
        scratch.append(pltpu.VMEM((tm, d), F32))
    kern = functools.partial(
        _outproj_kernel, part_refs=[p.n_refs() for p in parts], part_first=[p.first_tiles(tm) for p in parts],
        n_x=x.n_refs(), x_first=x.first_tiles(tm), d=d, tm=tm)
    return pl.pallas_call(
        kern,
        grid=(t // tm,),
        in_specs=in_specs,
        out_specs=[pl.BlockSpec((tm, d), lambda i: (i, 0)), pl.BlockSpec((tm, d), lambda i: (i, 0))],
        out_shape=[jax.ShapeDtypeStruct((t, d), F32), jax.ShapeDtypeStruct((t, d), BF16)],
        scratch_shapes=scratch,
        compiler_params=_params(("arbitrary",), 56),
        name="outproj",
    )(*args)


def _ffn_down_kernel(act_ref, w_ref, x_ref, gate_ref, g_ref, *rest, final, d, tm, n_k):
    if final:
        (acc_ref,) = rest
    else:
        sh_ref, sc_ref, acc_ref, h_ref = rest
    k = pl.program_id(1)

    @pl.when(k == 0)
    def _():
        for c0 in range(0, d, COL_CHUNK):
            acc_ref[:, c0:c0 + COL_CHUNK] = _dot(act_ref[...], w_ref[:, c0:c0 + COL_CHUNK])

    @pl.when(k > 0)
    def _():
        for c0 in range(0, d, COL_CHUNK):
            acc_ref[:, c0:c0 + COL_CHUNK] += _dot(act_ref[...], w_ref[:, c0:c0 + COL_CHUNK])

    @pl.when(k == n_k - 1)
    def _():
        inv = _residual_norm(lambda cols: acc_ref[:, cols], lambda cols: x_ref[:, cols], gate_ref, acc_ref, d, tm)
        if final:
            for c0 in range(0, d, COL_CHUNK):
                cols = slice(c0, c0 + COL_CHUNK)
                acc_ref[:, cols] = acc_ref[:, cols] * inv * g_ref[:, cols]
        else:
            _write_modulated(acc_ref, inv, g_ref, sh_ref, sc_ref, h_ref, d)


def _ffn_down(act, w, x, mod, mod_next, g_next, n_ctx, lat_len, row0=0, n_rows=None, tm=1024):
    t, d = x.shape
    f = act.shape[1]
    final = mod_next is None
    n_rows = t if n_rows is None else n_rows
    tk = f // 4
    assert f % tk == 0 and tk % LANES == 0 and row0 % tm == 0 and n_rows % tm == 0
    n_k = f // tk
    t0 = row0 // tm
    mod_spec = lambda which: pl.BlockSpec(
        (1, 1, 1, d), lambda i, k: (_group_of((i + t0) * tm, n_ctx, lat_len), which, 0, 0))
    in_specs = [
        pl.BlockSpec((tm, tk), lambda i, k: (i + t0, k)),
        pl.BlockSpec((tk, d), lambda i, k: (k, 0)),
        pl.BlockSpec((tm, d), lambda i, k: (i + t0, 0)),
        mod_spec(MOD_GATE2),
        pl.BlockSpec((1, d), lambda i, k: (0, 0)),
    ]
    args = [act, w, x, mod, g_next.reshape(1, d)]
    if final:
        out_specs = pl.BlockSpec((tm, d), lambda i, k: (i, 0))
        out_shape = jax.ShapeDtypeStruct((n_rows, d), F32)
    else:
        in_specs += [mod_spec(MOD_SHIFT1), mod_spec(MOD_SCALE1)]
        args += [mod_next, mod_next]
        out_specs = [pl.BlockSpec((tm, d), lambda i, k: (i, 0)), pl.BlockSpec((tm, d), lambda i, k: (i, 0))]
        out_shape = [jax.ShapeDtypeStruct((n_rows, d), F32), jax.ShapeDtypeStruct((n_rows, d), BF16)]
    kern = functools.partial(_ffn_down_kernel, final=final, d=d, tm=tm, n_k=n_k)
    return pl.pallas_call(
        kern,
        grid=(n_rows // tm, n_k),
        in_specs=in_specs,
        out_specs=out_specs,
        out_shape=out_shape,
        compiler_params=_params(("arbitrary", "arbitrary"), 60),
        name="ffn_down",
    )(*args)


HALO = BF16_SUBLANES


def _ffn_up_kernel(h_ref, hp_ref, hn_ref, wg_ref, wu_ref, cg_ref, cu_ref, bg_ref, bu_ref, *rest,
                   tm, subs, n_ctx, ctx_len, lat_len, n_sides):
    side_in = rest[:n_sides]
    o_ref = rest[n_sides]
    side_out = rest[n_sides + 1:2 * n_sides + 1]
    hb_ref = rest[2 * n_sides + 1]
    _run_side_casts(side_in, side_out)
    i = pl.program_id(0)
    row0 = i * (tm * subs)
    is_ctx = row0 < n_ctx
    assert ctx_len & (ctx_len - 1) == 0 and lat_len & (lat_len - 1) == 0 and n_ctx % lat_len == 0

    def seq_pos(r):
        return jnp.where(is_ctx, r & (ctx_len - 1), r & (lat_len - 1))

    def joined(r):
        return (seq_pos(r) != 0).astype(F32)

    @pl.when(pl.program_id(1) == 0)
    def _():
        for s in range(subs):
            lo = s * tm
            hb_ref[s, HALO:HALO + tm, :] = h_ref[lo:lo + tm, :]
            prev = hp_ref[...] if s == 0 else h_ref[lo - HALO:lo, :]
            nxt = hn_ref[...] if s == subs - 1 else h_ref[lo + tm:lo + tm + HALO, :]
            hb_ref[s, 0:HALO, :] = (prev.astype(F32) * joined(row0 + lo)).astype(BF16)
            hb_ref[s, HALO + tm:, :] = (nxt.astype(F32) * joined(row0 + lo + tm)).astype(BF16)

    tot = tm + 2 * HALO
    zs = [(_dot(hb_ref[s], wg_ref[...]), _dot(hb_ref[s], wu_ref[...])) for s in range(subs)]

    for s in range(subs):
        rows = row0 + s * tm + lax.broadcasted_iota(jnp.int32, (tm, 1), 0)
        keep_p = joined(rows)
        keep_n = joined(rows + 1)

        def conv(z, c_ref, b_ref):
            zp = pltpu.roll(z, 1, axis=0)[HALO:HALO + tm]
            zn = pltpu.roll(z, tot - 1, axis=0)[HALO:HALO + tm]
            c = c_ref[...]
            return (zp * keep_p) * c[0:1] + z[HALO:HALO + tm] * c[1:2] + (zn * keep_n) * c[2:3] + b_ref[...]

        gate = conv(zs[s][0], cg_ref, bg_ref)
        up = conv(zs[s][1], cu_ref, bu_ref)
        o_ref[s * tm:(s + 1) * tm, :] = (gate * _sigmoid(gate) * up).astype(BF16)


def _ffn_up(h, w, conv_w, conv_b, n_ctx, ctx_len, lat_len, tm=1024, subs=2, tn=512, side_weights=()):
    t, d = h.shape
    f = w.shape[1] // 2
    nj = f // tn
    bm = tm * subs
    assert n_ctx % bm == 0 and t % bm == 0
    hb = bm // HALO
    last_blk = t // HALO - 1
    sides = [_SideCast(sw, layer, (t // bm, nj)) for sw, layer in side_weights]
    conv_b = conv_b.reshape(1, 2 * f)
    kern = functools.partial(_ffn_up_kernel, tm=tm, subs=subs, n_ctx=n_ctx, ctx_len=ctx_len, lat_len=lat_len,
                             n_sides=len(sides))
    res = pl.pallas_call(
        kern,
        grid=(t // bm, nj),
        in_specs=[
            pl.BlockSpec((bm, d), lambda i, j: (i, 0)),
            pl.BlockSpec((HALO, d), lambda i, j: (jnp.maximum(i * hb - 1, 0), 0)),
            pl.BlockSpec((HALO, d), lambda i, j: (jnp.minimum((i + 1) * hb, last_blk), 0)),
            pl.BlockSpec((d, tn), lambda i, j: (0, j)),
            pl.BlockSpec((d, tn), lambda i, j: (0, j + nj)),
            pl.BlockSpec((3, tn), lambda i, j: (0, j)),
            pl.BlockSpec((3, tn), lambda i, j: (0, j + nj)),
            pl.BlockSpec((1, tn), lambda i, j: (0, j)),
            pl.BlockSpec((1, tn), lambda i, j: (0, j + nj)),
        ] + [s.in_spec() for s in sides],
        out_specs=[pl.BlockSpec((bm, tn), lambda i, j: (i, j))] + [s.out_spec() for s in sides],
        out_shape=[jax.ShapeDtypeStruct((t, f), BF16)] + [s.out_shape() for s in sides],
        scratch_shapes=[pltpu.VMEM((subs, tm + 2 * HALO, d), BF16)],
        compiler_params=_params(("arbitrary", "arbitrary"), 56),
        name="ffn_up",
    )(h, h, h, w, w, conv_w, conv_w, conv_b, conv_b, *[s.w for s in sides])
    return list(res)


def _attend(score_blocks, value_blocks, sink=None):
    m = jnp.max(score_blocks[0], axis=-1, keepdims=True)
    for s in score_blocks[1:]:
        m = jnp.maximum(m, jnp.max(s, axis=-1, keepdims=True))
    if sink is not None:
        m = jnp.maximum(m, sink)
    den = jnp.exp(sink - m) if sink is not None else 0.0
    out = None
    for s, v in zip(score_blocks, value_blocks):
        p = jnp.exp(s - m)
        den = den + jnp.sum(p, axis=-1, keepdims=True)
        o = _dot(p.astype(BF16), v)
        out = o if out is None else out + o
    return out / den


def _scaled_q(q):
    return (q * ATT_SCALE).astype(BF16)


CTX_SEQS_PER_STEP = 4


def _ctx_attn_kernel(*refs, n_heads, n_kv, has_sink, seq_len):
    if has_sink:
        sink_ref, q_ref, k_ref, v_ref, o_ref = refs
    else:
        q_ref, k_ref, v_ref, o_ref = refs
    grp = n_heads // n_kv

    def one_sequence(s, carry):
        rows = pl.ds(pl.multiple_of(s * seq_len, seq_len), seq_len)
        for kv in range(n_kv):
            ksl = slice(kv * HEAD_DIM, (kv + 1) * HEAD_DIM)
            k = k_ref[rows, ksl].astype(BF16)
            v = v_ref[rows, ksl].astype(BF16)
            for gi in range(grp):
                h = kv * grp + gi
                hsl = slice(h * HEAD_DIM, (h + 1) * HEAD_DIM)
                sc = _dot_t(q_ref[rows, hsl], k)
                sink = sink_ref[h] if has_sink else None
                o_ref[rows, hsl] = _attend([sc], [v], sink).astype(BF16)
        return carry

    lax.fori_loop(0, CTX_SEQS_PER_STEP, one_sequence, 0)


def _ctx_attn(q, k, v, n_seq, seq_len, n_heads, n_kv, sink=None):
    qw = n_heads * HEAD_DIM
    kw = n_kv * HEAD_DIM
    has_sink = sink is not None
    assert n_seq % CTX_SEQS_PER_STEP == 0
    blk = CTX_SEQS_PER_STEP * seq_len
    in_specs = [
        pl.BlockSpec((blk, qw), lambda b: (b, 0)),
        pl.BlockSpec((blk, kw), lambda b: (b, 0)),
        pl.BlockSpec((blk, kw), lambda b: (b, 0)),
    ]
    args = [q, k, v]
    if has_sink:
        in_specs = [pl.BlockSpec(memory_space=pltpu.SMEM)] + in_specs
        args = [sink] + args
    return pl.pallas_call(
        functools.partial(_ctx_attn_kernel, n_heads=n_heads, n_kv=n_kv, has_sink=has_sink, seq_len=seq_len),
        grid=(n_seq // CTX_SEQS_PER_STEP,),
        in_specs=in_specs,
        out_specs=pl.BlockSpec((blk, qw), lambda b: (b, 0)),
        out_shape=jax.ShapeDtypeStruct((n_seq * seq_len, qw), BF16),
        compiler_params=_params(("arbitrary",), 40),
        name="ctx_attn",
    )(*args)


NA_QROWS = 8
NA_KROWS = 16
NA_QBLK = NA_QROWS * GRID_W
NA_KBLK = NA_KROWS * GRID_W
NA_HEADS_PER_STEP = 4


def _na_key_row0(blk, rows):
    return np.clip(blk * NA_QROWS - NA_WIN_R // 2, 0, rows - NA_KROWS)


def _na_row_offsets(rows):
    nblk = rows // NA_QROWS
    a_idx = np.full((3, NA_QROWS, NA_KROWS), -1, np.int32)
    for v, blk in enumerate((0, 1, nblk - 1)):
        kr0 = _na_key_row0(blk, rows)
        for ql in range(NA_QROWS):
            qr = blk * NA_QROWS + ql
            start = np.clip(qr - NA_WIN_R // 2, 0, rows - NA_WIN_R)
            for kl in range(NA_KROWS):
                kr = kr0 + kl
                if start <= kr < start + NA_WIN_R:
                    a_idx[v, ql, kl] = kr - qr + NA_WIN_R - 1
    return a_idx


def _na_bias_kernel(rpb_ref, o_ref, *, a_idx):
    h = pl.program_id(0)
    n_dr = 2 * NA_WIN_R - 1
    n_dc = 2 * NA_WIN_C - 1
    qc = lax.broadcasted_iota(jnp.int32, (GRID_W, GRID_W), 0)
    kc = lax.broadcasted_iota(jnp.int32, (GRID_W, GRID_W), 1)
    col_start = jnp.clip(qc - NA_WIN_C // 2, 0, GRID_W - NA_WIN_C)
    col_ok = (kc >= col_start) & (kc < col_start + NA_WIN_C)
    dc = jnp.clip(kc - qc + NA_WIN_C - 1, 0, n_dc - 1)
    is_dc = [dc == j for j in range(n_dc)]
    masked = jnp.full((GRID_W, GRID_W), NEG_INF, F32)
    tiles = []
    for a in range(n_dr):
        t = jnp.zeros((GRID_W, GRID_W), F32)
        for j in range(n_dc):
            t = jnp.where(is_dc[j], rpb_ref[(h * n_dr + a) * n_dc + j], t)
        tiles.append(jnp.where(col_ok, t, NEG_INF))
    for v in range(a_idx.shape[0]):
        for ql in range(NA_QROWS):
            for kl in range(NA_KROWS):
                a = int(a_idx[v, ql, kl])
                o_ref[v, 0, ql * GRID_W:(ql + 1) * GRID_W, kl * GRID_W:(kl + 1) * GRID_W] = (
                    tiles[a] if a >= 0 else masked)


def _na_bias_tables(rpb, rows):
    h = rpb.shape[0]
    a_idx = _na_row_offsets(rows)
    return pl.pallas_call(
        functools.partial(_na_bias_kernel, a_idx=a_idx),
        grid=(h,),
        in_specs=[pl.BlockSpec(memory_space=pltpu.SMEM)],
        out_specs=pl.BlockSpec((3, 1, NA_QBLK, NA_KBLK), lambda hh: (0, hh, 0, 0)),
        out_shape=jax.ShapeDtypeStruct((3, h, NA_QBLK, NA_KBLK), F32),
        compiler_params=_params(("arbitrary",), 32),
        name="na_bias",
    )(rpb.astype(F32).reshape(-1))


def _na_kernel(q_ref, k_ref, v_ref, kc_ref, vc_ref, bias_ref, o_ref, *, rows):
    i = pl.program_id(2)
    kr0 = jnp.clip(i * NA_QROWS - NA_WIN_R // 2, 0, rows - NA_KROWS)
    k0 = pl.multiple_of(kr0 * GRID_W, GRID_W * (NA_WIN_R // 2))
    for hh in range(NA_HEADS_PER_STEP):
        hsl = slice(hh * HEAD_DIM, (hh + 1) * HEAD_DIM)
        q = q_ref[:, hsl]
        kw = k_ref[pl.ds(k0, NA_KBLK), hsl]
        vw = v_ref[pl.ds(k0, NA_KBLK), hsl]
        s_loc = _dot_t(q, kw) + bias_ref[0, hh]
        s_ctx = _dot_t(q, kc_ref[0, :, hsl].astype(BF16))
        o_ref[:, hsl] = _attend([s_loc, s_ctx], [vw, vc_ref[0, :, hsl].astype(BF16)]).astype(BF16)


def _na_attn(q, k, v, cache_k, cache_v, bias, n_ctx, n_lat, lat_len):
    rows = lat_len // GRID_W
    nblk = lat_len // NA_QBLK
    assert n_ctx % NA_QBLK == 0
    qb0 = n_ctx // NA_QBLK
    hps = NA_HEADS_PER_STEP
    hw = hps * HEAD_DIM
    hg = NA_HEADS // hps
    c = cache_k.shape[1]
    variant = lambda i: jnp.where(i == 0, 0, jnp.where(i == nblk - 1, 2, 1))
    return pl.pallas_call(
        functools.partial(_na_kernel, rows=rows),
        grid=(n_lat, hg, nblk),
        in_specs=[
            pl.BlockSpec((NA_QBLK, hw), lambda b, hh, i: (qb0 + b * nblk + i, hh)),
            pl.BlockSpec((lat_len, hw), lambda b, hh, i: (b, hh)),
            pl.BlockSpec((lat_len, hw), lambda b, hh, i: (b, hh)),
            pl.BlockSpec((1, c, hw), lambda b, hh, i: (b, 0, hh)),
            pl.BlockSpec((1, c, hw), lambda b, hh, i: (b, 0, hh)),
            pl.BlockSpec((1, hps, NA_QBLK, NA_KBLK), lambda b, hh, i: (variant(i), hh, 0, 0)),
        ],
        out_specs=pl.BlockSpec((NA_QBLK, hw), lambda b, hh, i: (b * nblk + i, hh)),
        out_shape=jax.ShapeDtypeStruct((n_lat * lat_len, NA_HEADS * HEAD_DIM), BF16),
        compiler_params=_params(("arbitrary", "arbitrary", "arbitrary"), 56),
        name="na_attn",
    )(q, k, v, cache_k, cache_v, bias)


SWA_QBLK = 512
SWA_KBLK = SWA_QBLK + 2 * SWA_WINDOW


def _rope_tables(length):
    half = HEAD_DIM // 2
    freqs = ROPE_BASE ** (-jnp.arange(0, half, 2, dtype=F32) / half)
    t = jnp.arange(length)
    ang_r = (t // GRID_W).astype(F32)[:, None] * freqs[None, :]
    ang_c = (t % GRID_W).astype(F32)[:, None] * freqs[None, :]
    ang = jnp.concatenate([ang_r, ang_r, ang_c, ang_c], axis=1)
    sign = np.concatenate([-np.ones(half // 2), np.ones(half // 2)] * 2).astype(np.float32)
    return jnp.cos(ang), jnp.sin(ang) * jnp.asarray(sign)[None, :]


def _rope(x, cos, sin_signed):
    lane = lax.broadcasted_iota(jnp.int32, x.shape, 1)
    quarter = HEAD_DIM // 4
    lower = (lane & (2 * quarter - 1)) < quarter
    swapped = jnp.where(lower, pltpu.roll(x, HEAD_DIM - quarter, axis=1), pltpu.roll(x, quarter, axis=1))
    return x * cos + swapped * sin_signed


def _swa_kernel(sink_ref, q_ref, k_ref, v_ref, kc_ref, vc_ref, cos_ref, sin_ref, o_ref, *, lat_len):
    kv = pl.program_id(1)
    i = pl.program_id(2)
    grp = SWA_HEADS // SWA_KV_HEADS
    q0 = pl.multiple_of(i * SWA_QBLK, SWA_QBLK)
    k0 = pl.multiple_of(jnp.clip(i * SWA_QBLK - SWA_WINDOW, 0, lat_len - SWA_KBLK), SWA_WINDOW)
    kw = _rope(k_ref[pl.ds(k0, SWA_KBLK), :], cos_ref[pl.ds(k0, SWA_KBLK), :],
               sin_ref[pl.ds(k0, SWA_KBLK), :]).astype(BF16)
    vw = v_ref[pl.ds(k0, SWA_KBLK), :].astype(BF16)
    kc = kc_ref[0].astype(BF16)
    vc = vc_ref[0].astype(BF16)
    qpos = q0 + lax.broadcasted_iota(jnp.int32, (SWA_QBLK, SWA_KBLK), 0)
    kpos = k0 + lax.broadcasted_iota(jnp.int32, (SWA_QBLK, SWA_KBLK), 1)
    band = jnp.where(jnp.abs(qpos - kpos) <= SWA_WINDOW, 0.0, NEG_INF)
    cos_q = cos_ref[pl.ds(q0, SWA_QBLK), :]
    sin_q = sin_ref[pl.ds(q0, SWA_QBLK), :]
    for gi in range(grp):
        hsl = slice(gi * HEAD_DIM, (gi + 1) * HEAD_DIM)
        q = _scaled_q(_rope(q_ref[:, hsl], cos_q, sin_q))
        s_loc = _dot_t(q, kw) + band
        s_ctx = _dot_t(q, kc)
        sink = sink_ref[kv * grp + gi]
        o_ref[:, hsl] = _attend([s_loc, s_ctx], [vw, vc], sink).astype(BF16)


def _swa_attn(q, kv, cache_k, cache_v, sink, cos, sin_signed, n_lat, lat_len):
    grp = SWA_HEADS // SWA_KV_HEADS
    gw = grp * HEAD_DIM
    nblk = lat_len // SWA_QBLK
    c = cache_k.shape[1]
    return pl.pallas_call(
        functools.partial(_swa_kernel, lat_len=lat_len),
        grid=(n_lat, SWA_KV_HEADS, nblk),
        in_specs=[
            pl.BlockSpec(memory_space=pltpu.SMEM),
            pl.BlockSpec((SWA_QBLK, gw), lambda b, kv, i: (b * nblk + i, kv)),
            pl.BlockSpec((lat_len, HEAD_DIM), lambda b, kv, i: (b, kv)),
            pl.BlockSpec((lat_len, HEAD_DIM), lambda b, kv, i: (b, SWA_KV_HEADS + kv)),
            pl.BlockSpec((1, c, HEAD_DIM), lambda b, kv, i: (b, 0, kv)),
            pl.BlockSpec((1, c, HEAD_DIM), lambda b, kv, i: (b, 0, kv)),
            pl.BlockSpec((lat_len, HEAD_DIM), lambda b, kv, i: (0, 0)),
            pl.BlockSpec((lat_len, HEAD_DIM), lambda b, kv, i: (0, 0)),
        ],
        out_specs=pl.BlockSpec((SWA_QBLK, gw), lambda b, kv, i: (b * nblk + i, kv)),
        out_shape=jax.ShapeDtypeStruct((n_lat * lat_len, SWA_HEADS * HEAD_DIM), BF16),
        compiler_params=_params(("arbitrary", "arbitrary", "arbitrary"), 48),
        name="swa_attn",
    )(sink, q, kv, kv, cache_k, cache_v, cos, sin_signed)


def _s5_disc_kernel(are_ref, aim_ref, ldt_ref, bre_ref, bim_ref, abre_ref, abim_ref, bbre_ref, bbim_ref):
    a_re = are_ref[...]
    a_im = aim_ref[...]
    dt = jnp.exp(ldt_ref[...])
    mag = jnp.exp(a_re * dt)
    ang = a_im * dt
    ab_re = mag * jnp.cos(ang)
    ab_im = mag * jnp.sin(ang)
    den = a_re * a_re + a_im * a_im
    n_re = ab_re - 1.0
    f_re = (n_re * a_re + ab_im * a_im) / den
    f_im = (ab_im * a_re - n_re * a_im) / den
    b_re = bre_ref[...]
    b_im = bim_ref[...]
    abre_ref[...] = ab_re
    abim_ref[...] = ab_im
    bbre_ref[...] = f_re * b_re - f_im * b_im
    bbim_ref[...] = f_re * b_im + f_im * b_re


def _s5_discretise(a_re, a_im, log_dt, b_re, b_im):
    nd, g, p, c = b_re.shape
    shape2 = (nd * g, p * c)
    expand = lambda a: jnp.broadcast_to(a[..., None], (nd, g, p, c)).reshape(shape2)
    ldt = jnp.broadcast_to(log_dt[:, :, None, None], (nd, g, p, c)).reshape(shape2)
    out = pl.pallas_call(
        _s5_disc_kernel,
        out_shape=[jax.ShapeDtypeStruct(shape2, F32)] * 4,
        compiler_params=_params(None, 16),
        name="s5_discretise",
    )(expand(a_re), expand(a_im), ldt, b_re.reshape(shape2), b_im.reshape(shape2))
    ab_re, ab_im, bb_re, bb_im = [o.reshape(nd, g, p, c) for o in out]
    return ab_re[..., 0], ab_im[..., 0], bb_re, bb_im


def _block_diag(m, gb):
    g, a, b = m.shape
    m = m.reshape(g // gb, gb, a, b)
    eye = jnp.eye(gb, dtype=m.dtype)
    out = m[:, :, :, None, :] * eye[None, :, None, :, None]
    return out.reshape(g // gb, gb * a, gb * b)


def _cmul_add(ar, ai, xr, xi, br, bi):
    return ar * xr - ai * xi + br, ar * xi + ai * xr + bi


def _s5_kernel(*refs, latent):
    if latent:
        (u_ref, bbr_ref, bbi_ref, ccr_ref, cci_ref, ar_ref, ai_ref, d_ref, h0_ref,
         y_ref, xr_s, xi_s, cr_s, ci_s) = refs
    else:
        (u_ref, bbr_ref, bbi_ref, ccr_ref, cci_ref, ar_ref, ai_ref, d_ref,
         y_ref, fin_ref, xr_s, xi_s) = refs
    rows, steps = S5_ROWS, S5_STEPS
    u = u_ref[0]
    ub = u.astype(BF16)
    y = d_ref[0] * u
    for dirn in range(2):
        reverse = dirn == 1
        xr_s[...] = _dot(ub, bbr_ref[dirn, 0])
        xi_s[...] = _dot(ub, bbi_ref[dirn, 0])
        a_r1 = ar_ref[dirn, 0]
        a_i1 = ai_ref[dirn, 0]
        a_r = jnp.broadcast_to(a_r1, (rows, S5_GB_ST))
        a_i = jnp.broadcast_to(a_i1, (rows, S5_GB_ST))

        def time_rows(s):
            t = (steps - 1 - s) if reverse else s
            if isinstance(t, int):
                return pl.ds(t * rows, rows)
            return pl.ds(pl.multiple_of(t * rows, rows), rows)

        first = time_rows(0)
        final = time_rows(steps - 1)

        def scan_body(s, carry):
            pr, pi = carry
            sl = time_rows(s)
            nr, ni = _cmul_add(a_r, a_i, pr, pi, xr_s[sl, :], xi_s[sl, :])
            xr_s[sl, :] = nr
            xi_s[sl, :] = ni
            return nr, ni

        lax.fori_loop(1, steps, scan_body, (xr_s[first, :], xi_s[first, :]), unroll=4)

        if not latent:
            fin_ref[0, dirn, 0] = xr_s[final, :]
            fin_ref[0, dirn, 1] = xi_s[final, :]
        else:
            p_r, p_i = a_r1, a_i1
            for _ in range(int(math.log2(steps))):
                p_r, p_i = p_r * p_r - p_i * p_i, 2.0 * p_r * p_i
            assert 1 << int(math.log2(steps)) == steps
            e_r = xr_s[final, :]
            e_i = xi_s[final, :]
            c_r = h0_ref[0, 2 * dirn]
            c_i = h0_ref[0, 2 * dirn + 1]
            order = range(rows - 1, -1, -1) if reverse else range(rows)
            for j in order:
                cr_s[j:j + 1, :] = c_r
                ci_s[j:j + 1, :] = c_i
                c_r, c_i = _cmul_add(p_r, p_i, c_r, c_i, e_r[j:j + 1], e_i[j:j + 1])
            zero = jnp.zeros((rows, S5_GB_ST), F32)
            d_r, d_i = _cmul_add(a_r, a_i, cr_s[...], ci_s[...], zero, zero)

            def fix_body(s, carry):
                dr, di = carry
                sl = time_rows(s)
                xr_s[sl, :] = xr_s[sl, :] + dr
                xi_s[sl, :] = xi_s[sl, :] + di
                return _cmul_add(a_r, a_i, dr, di, zero, zero)

            lax.fori_loop(0, steps, fix_body, (d_r, d_i), unroll=4)

        y = y + _dot(xr_s[...].astype(BF16), ccr_ref[dirn, 0]) - _dot(xi_s[...].astype(BF16), cci_ref[dirn, 0])
    y_ref[0] = y


def _s5_scan(u_tm, tile0, tiles, bbr, bbi, ccr, cci, ab_re, ab_im, d, h0=None):
    _, tr, w = u_tm.shape
    assert tr == S5_STEPS * S5_ROWS
    nb = w // S5_GB_IN
    latent = h0 is not None
    in_specs = [
        pl.BlockSpec((1, tr, S5_GB_IN), lambda t, gb: (tile0 + t, 0, gb)),
        pl.BlockSpec((2, 1, S5_GB_IN, S5_GB_ST), lambda t, gb: (0, gb, 0, 0)),
        pl.BlockSpec((2, 1, S5_GB_IN, S5_GB_ST), lambda t, gb: (0, gb, 0, 0)),
        pl.BlockSpec((2, 1, S5_GB_ST, S5_GB_IN), lambda t, gb: (0, gb, 0, 0)),
        pl.BlockSpec((2, 1, S5_GB_ST, S5_GB_IN), lambda t, gb: (0, gb, 0, 0)),
        pl.BlockSpec((2, 1, 1, S5_GB_ST), lambda t, gb: (0, gb, 0, 0)),
        pl.BlockSpec((2, 1, 1, S5_GB_ST), lambda t, gb: (0, gb, 0, 0)),
        pl.BlockSpec((1, 1, S5_GB_IN), lambda t, gb: (gb, 0, 0)),
    ]
    args = [u_tm, bbr, bbi, ccr, cci, ab_re, ab_im, d]
    y_spec = pl.BlockSpec((1, tr, S5_GB_IN), lambda t, gb: (t, 0, gb))
    y_shape = jax.ShapeDtypeStruct((tiles, tr, w), F32)
    scratch = [pltpu.VMEM((tr, S5_GB_ST), F32), pltpu.VMEM((tr, S5_GB_ST), F32)]
    if latent:
        in_specs.append(pl.BlockSpec((1, 4, 1, S5_GB_ST), lambda t, gb: (t, 0, 0, gb)))
        args.append(h0)
        out_specs, out_shape = y_spec, y_shape
        scratch += [pltpu.VMEM((S5_ROWS, S5_GB_ST), F32), pltpu.VMEM((S5_ROWS, S5_GB_ST), F32)]
    else:
        out_specs = [y_spec, pl.BlockSpec((1, 2, 2, S5_ROWS, S5_GB_ST), lambda t, gb: (t, 0, 0, 0, gb))]
        out_shape = [y_shape, jax.ShapeDtypeStruct((tiles, 2, 2, S5_ROWS, nb * S5_GB_ST), F32)]
    return pl.pallas_call(
        functools.partial(_s5_kernel, latent=latent),
        grid=(tiles, nb),
        in_specs=in_specs,
        out_specs=out_specs,
        out_shape=out_shape,
        scratch_shapes=scratch,
        compiler_params=_params(("arbitrary", "arbitrary"), 48),
        name="s5_scan_lat" if latent else "s5_scan_ctx",
    )(*args)


def _glu_kernel(*refs, n_y, first_tiles):
    y_refs = refs[:n_y]
    w_ref, b_ref, o_ref = refs[n_y:]

    def run(y_ref):
        y = y_ref[...]
        g = 0.5 * y * (1.0 + jnp.tanh(math.sqrt(2.0 / math.pi) * (y + 0.044715 * (y * y * y))))
        z = _dot(g.astype(BF16), w_ref[...]) + b_ref[...]
        o_ref[...] = (g * _sigmoid(z)).astype(BF16)

    _select_rows(y_refs, pl.program_id(0), first_tiles, run)


def _glu(y, w, b, tm=1024):
    t, n = y.rows, y.width
    return pl.pallas_call(
        functools.partial(_glu_kernel, n_y=y.n_refs(), first_tiles=y.first_tiles(tm)),
        grid=(t // tm,),
        in_specs=y.specs(tm) + [
            pl.BlockSpec((n, n), lambda i: (0, 0)),
            pl.BlockSpec((1, n), lambda i: (0, 0)),
        ],
        out_specs=pl.BlockSpec((tm, n), lambda i: (i, 0)),
        out_shape=jax.ShapeDtypeStruct((t, n), BF16),
        compiler_params=_params(("arbitrary",), 32),
        name="s5_glu",
    )(*y.arrays, w, b.reshape(1, n))


def _s5_mixer(u, p, state, n_ctx_seq, ctx_len, n_lat, lat_len):
    t, w = u.shape
    g = w // S5_GROUP
    assert ctx_len == S5_STEPS and lat_len == S5_STEPS * S5_ROWS and n_ctx_seq % S5_ROWS == 0
    ab_re, ab_im, bb_re, bb_im = _s5_discretise(p["a_re"], p["a_im"], p["log_dt"], p["b_re"], p["b_im"])
    nb = g // S5_GB
    blocks = lambda m: jnp.stack([_block_diag(m[d].transpose(0, 2, 1), S5_GB) for d in range(2)]).astype(BF16)
    bbr, bbi = blocks(bb_re), blocks(bb_im)
    ccr, cci = blocks(p["c_re"].astype(F32)), blocks(p["c_im"].astype(F32))
    a_r = ab_re.reshape(2, nb, 1, S5_GB_ST)
    a_i = ab_im.reshape(2, nb, 1, S5_GB_ST)
    d = p["d"].astype(F32).reshape(nb, 1, S5_GB_IN)
    tiles = t // (S5_ROWS * S5_STEPS)
    ctx_tiles = n_ctx_seq // S5_ROWS
    tr = S5_STEPS * S5_ROWS
    u_tm = u.reshape(tiles, S5_ROWS, S5_STEPS, w).transpose(0, 2, 1, 3).reshape(tiles, tr, w)
    y_ctx, fin = _s5_scan(u_tm, 0, ctx_tiles, bbr, bbi, ccr, cci, a_r, a_i, d)
    h0 = state.astype(F32).transpose(0, 1, 4, 2, 3).reshape(n_lat, 4, 1, g * S5_STATE)
    y_lat = _s5_scan(u_tm, ctx_tiles, tiles - ctx_tiles, bbr, bbi, ccr, cci, a_r, a_i, d, h0=h0)
    y = _Rows(y_ctx.reshape(ctx_tiles * tr, w), y_lat.reshape((tiles - ctx_tiles) * tr, w))
    o = _glu(y, p["glu_w"].astype(BF16), p["glu_b"].astype(F32))
    o = o.reshape(tiles, S5_STEPS, S5_ROWS, w).transpose(0, 2, 1, 3).reshape(t, w)
    fin = fin.transpose(0, 3, 1, 4, 2).reshape(n_ctx_seq, 2, g, S5_STATE, 2)
    return o, fin


def kernel(x_prompt, x_sample, c, cache_na_k, cache_na_v, state_s5, cache_swa_k, cache_swa_v, c_ctx, norm_mix, norm_ffn, ada_w, ada_b, ab_w_in, ab_w_out, na_rpb, s5_a_re, s5_a_im, s5_log_dt, s5_b_re, s5_b_im, s5_c_re, s5_c_im, s5_d, s5_glu_w, s5_glu_b, swa_w_in, swa_w_out, swa_sink, ffn_w_up, ffn_conv_w, ffn_conv_b, ffn_w_down, final_norm):
    bp, lp, d = x_prompt.shape
    bs, ls, _ = x_sample.shape
    depth = ada_w.shape[0]
    n_ctx = bp * lp
    n_lat = bs * ls
    na_w = NA_HEADS * HEAD_DIM
    swa_q = SWA_HEADS * HEAD_DIM
    swa_kv = SWA_KV_HEADS * HEAD_DIM
    assert depth == 2, "layer pattern below is written for one A/B layer followed by one C layer"

    cond = jnp.concatenate([c_ctx[None, :], c, jnp.zeros((SUBLANES - 1 - bs, d), F32)], axis=0)
    mods = _adaln(cond, ada_w, ada_b).reshape(depth, SUBLANES, 6, 1, d)
    cos, sin_signed = _rope_tables(ls)
    tn = 1024
    assert na_w == tn and swa_q == 2 * tn and 2 * swa_kv == tn

    x0 = _Rows(x_prompt.reshape(n_ctx, d), x_sample.reshape(n_lat, d))
    q, k_ctx, k_lat, v_ctx, v_lat, u, w_up0 = _proj(
        _norm_mod(x0, mods[0], norm_mix[0], n_ctx, ls), _to_bf16(ab_w_in[0]),
        [_ProjOut(0, 1, "all", BF16, scale=ATT_SCALE),
         _ProjOut(1, 2, "ctx", F32), _ProjOut(1, 2, "lat", BF16),
         _ProjOut(2, 3, "ctx", F32), _ProjOut(2, 3, "lat", BF16),
         _ProjOut(3, 4, "all", F32)], n_ctx, tn=tn, side_weights=[(ffn_w_up, 0)])
    att_ctx = _ctx_attn(q, k_ctx, v_ctx, bp, lp, NA_HEADS, NA_HEADS)
    bias = _na_bias_tables(na_rpb[0], ls // GRID_W)
    att_lat = _na_attn(q, k_lat, v_lat, cache_na_k[:, 0].reshape(bs, -1, na_w),
                       cache_na_v[:, 0].reshape(bs, -1, na_w), bias, n_ctx, bs, ls)
    s5p = dict(a_re=s5_a_re[0], a_im=s5_a_im[0], log_dt=s5_log_dt[0], b_re=s5_b_re[0], b_im=s5_b_im[0],
               c_re=s5_c_re[0], c_im=s5_c_im[0], d=s5_d[0], glu_w=s5_glu_w[0], glu_b=s5_glu_b[0])
    s5o, s5_fin = _s5_mixer(u, s5p, state_s5[:, 0], bp, lp, bs, ls)
    x, h = _outproj([_Rows(att_ctx, att_lat), _Rows(s5o)], _to_bf16(ab_w_out[0]), x0, mods[0], norm_ffn[0],
                    n_ctx, ls)
    na_k = k_ctx.reshape(bp, 1, lp, NA_HEADS, HEAD_DIM)
    na_v = v_ctx.reshape(bp, 1, lp, NA_HEADS, HEAD_DIM)
    act, w_up1, w_down0, w_down1 = _ffn_up(h, w_up0, ffn_conv_w[0], ffn_conv_b[0], n_ctx, lp, ls,
                                           side_weights=[(ffn_w_up, 1), (ffn_w_down, 0), (ffn_w_down, 1)])
    x, h = _ffn_down(act, w_down0, x, mods[0], mods[1], norm_mix[1], n_ctx, ls)

    q_ctx, q_lat, k_ctx, v_ctx, kv_lat = _proj(
        h, _to_bf16(swa_w_in[0]),
        [_ProjOut(0, 2, "ctx", BF16, scale=ATT_SCALE), _ProjOut(0, 2, "lat", F32),
         _ProjOut(2, 3, "ctx", F32, cols=(0, swa_kv)), _ProjOut(2, 3, "ctx", F32, cols=(swa_kv, 2 * swa_kv)),
         _ProjOut(2, 3, "lat", F32)], n_ctx, tn=tn)
    sink = swa_sink[0].astype(F32)
    att_ctx = _ctx_attn(q_ctx, k_ctx, v_ctx, bp, lp, SWA_HEADS, SWA_KV_HEADS, sink=sink)
    att_lat = _swa_attn(q_lat, kv_lat, cache_swa_k[:, 0].reshape(bs, -1, swa_kv),
                        cache_swa_v[:, 0].reshape(bs, -1, swa_kv), sink, cos, sin_signed, bs, ls)
    x, h = _outproj([_Rows(att_ctx, att_lat)], _to_bf16(swa_w_out[0]), _Rows(x), mods[1], norm_ffn[1], n_ctx, ls)
    swa_k = k_ctx.reshape(bp, 1, lp, SWA_KV_HEADS, HEAD_DIM)
    swa_v = v_ctx.reshape(bp, 1, lp, SWA_KV_HEADS, HEAD_DIM)
    (act,) = _ffn_up(h, w_up1, ffn_conv_w[1], ffn_conv_b[1], n_ctx, lp, ls)
    y_p = _ffn_down(act, w_down1, x, mods[1], None, final_norm, n_ctx, ls, row0=0, n_rows=n_ctx)
    y_s = _ffn_down(act, w_down1, x, mods[1], None, final_norm, n_ctx, ls, row0=n_ctx, n_rows=n_lat)

    return (y_p.reshape(bp, lp, d), y_s.reshape(bs, ls, d), na_k, na_v, s5_fin[:, None], swa_k, swa_v)
```
